```python
import math
import jax, jax.numpy as jnp
from jax import lax
import numpy as np

D_MODEL = 1024
BATCH = 8
SEQ = 8192
DEPTH = 1

EPS = 1e-6
POOL_WINDOWS = (2, 4, 8, 16)
POOL_GROUPS = 4
POOL_WIDTH = D_MODEL // 2
POOL_GROUP_DIM = POOL_WIDTH // POOL_GROUPS
DN_HEADS = 8
DN_HEAD_DIM = 128
DN_WIDTH = DN_HEADS * DN_HEAD_DIM
CONV_WIDTH = 4
CHUNK = 64
PEER_HEADS = 8
N_KEYS = 128
N_EXPERTS = N_KEYS * N_KEYS
PEER_TOPK = 16
PEER_HALF = 128
PEER_QUERY_DIM = 2 * PEER_HALF
PEER_BLOCK = 64
SPLIT_POINTS = (
    POOL_WIDTH,
    POOL_WIDTH + 3 * DN_WIDTH,
    POOL_WIDTH + 4 * DN_WIDTH,
    POOL_WIDTH + 4 * DN_WIDTH + DN_HEADS,
    POOL_WIDTH + 4 * DN_WIDTH + 2 * DN_HEADS,
    POOL_WIDTH + 4 * DN_WIDTH + 2 * DN_HEADS + D_MODEL,
)
IN_COLS = POOL_WIDTH + 4 * DN_WIDTH + 2 * DN_HEADS + 2 * D_MODEL

kernel_name = "hybrid_pool_deltanet_peer_block"


def rmsnorm(x, w):
    xf = x.astype(jnp.float32)
    y = xf * lax.rsqrt(jnp.mean(xf * xf, axis=-1, keepdims=True) + EPS)
    return (y * w.astype(jnp.float32)).astype(x.dtype)


def l2norm(x):
    return x * lax.rsqrt(jnp.sum(x * x, axis=-1, keepdims=True) + EPS)


def pool_mixer(xa, pool_w, pool_scale):
    b, s, _ = xa.shape
    xf = xa.astype(jnp.float32)
    csum = jnp.cumsum(xf, axis=1)
    count = jnp.arange(1, s + 1, dtype=jnp.float32)[None, :, None]
    groups = []
    for g, win in enumerate(POOL_WINDOWS):
        sl = slice(g * POOL_GROUP_DIM, (g + 1) * POOL_GROUP_DIM)
        c = csum[..., sl]
        c_lag = jnp.pad(c, ((0, 0), (win, 0), (0, 0)))[:, :s]
        mean = (c - c_lag) / jnp.minimum(count, float(win))
        groups.append(mean - xf[..., sl])
    pooled = jnp.stack(groups, axis=2)
    y = jnp.einsum("bsgc,gcd->bsgd", pooled, pool_w.astype(jnp.float32))
    y = y.reshape(b, s, POOL_WIDTH) * pool_scale.astype(jnp.float32)
    return y.astype(xa.dtype)


def causal_dwconv(x, w):
    s = x.shape[1]
    k = w.shape[0]
    xp = jnp.pad(x, ((0, 0), (k - 1, 0), (0, 0)))
    y = xp[:, 0:s] * w[0]
    for j in range(1, k):
        y = y + xp[:, j:j + s] * w[j]
    return y


def gated_delta_rule(q, k, v, g, beta):
    b, h, s, dk = q.shape
    dv = v.shape[-1]
    nc = s // CHUNK
    q = q * (dk ** -0.5)
    k_beta = k * beta[..., None]
    v_beta = v * beta[..., None]

    def chunks(t):
        return t.reshape(b, h, nc, CHUNK, t.shape[-1])

    q, k, k_beta, v_beta = chunks(q), chunks(k), chunks(k_beta), chunks(v_beta)
    gc = jnp.cumsum(g.reshape(b, h, nc, CHUNK), axis=-1)
    tril = jnp.tril(jnp.ones((CHUNK, CHUNK), dtype=bool))
    strict = jnp.tril(jnp.ones((CHUNK, CHUNK), dtype=bool), -1)
    diff = gc[..., :, None] - gc[..., None, :]
    decay = jnp.where(tril, jnp.exp(jnp.where(tril, diff, 0.0)), 0.0)
    lower = jnp.where(strict, jnp.einsum("bhnid,bhnjd->bhnij", k_beta, k) * decay, 0.0)
    eye = jnp.eye(CHUNK, dtype=jnp.float32)
    t_inv = lax.linalg.triangular_solve(eye + lower, jnp.broadcast_to(eye, lower.shape),
                                        left_side=True, lower=True, unit_diagonal=True)
    u = jnp.einsum("bhnij,bhnjd->bhnid", t_inv, v_beta)
    w = jnp.einsum("bhnij,bhnjd->bhnid", t_inv, k_beta * jnp.exp(gc)[..., None])
    attn = jnp.where(tril, jnp.einsum("bhnid,bhnjd->bhnij", q, k) * decay, 0.0)

    def step(state, inp):
        q_i, k_i, u_i, w_i, a_i, g_i = inp
        v_new = u_i - jnp.einsum("bhcd,bhde->bhce", w_i, state)
        o = (jnp.einsum("bhcd,bhde->bhce", q_i * jnp.exp(g_i)[..., None], state)
             + jnp.einsum("bhij,bhje->bhie", a_i, v_new))
        g_last = g_i[..., -1]
        k_dec = k_i * jnp.exp(g_last[..., None] - g_i)[..., None]
        state = state * jnp.exp(g_last)[..., None, None] + jnp.einsum("bhcd,bhce->bhde", k_dec, v_new)
        return state, o

    xs = tuple(jnp.moveaxis(t, 2, 0) for t in (q, k, u, w, attn, gc))
    state0 = jnp.zeros((b, h, dk, dv), dtype=jnp.float32)
    _, o = lax.scan(step, state0, xs)
    return jnp.moveaxis(o, 0, 2).reshape(b, h, s, dv)


def deltanet_branch(qkv, z, beta_logit, a_logit, conv_w, a_log, dt_bias, dn_norm_w):
    b, s, _ = qkv.shape
    out_dtype = qkv.dtype
    f32 = jnp.float32
    qkv = jax.nn.silu(causal_dwconv(qkv.astype(f32), conv_w.astype(f32)))
    q, k, v = jnp.split(qkv, 3, axis=-1)

    def heads(t):
        return t.reshape(b, s, DN_HEADS, DN_HEAD_DIM).transpose(0, 2, 1, 3)

    q = l2norm(heads(q))
    k = l2norm(heads(k))
    v = heads(v)
    beta = jax.nn.sigmoid(beta_logit.astype(f32)).transpose(0, 2, 1)
    g = (-jnp.exp(a_log.astype(f32))
         * jax.nn.softplus(a_logit.astype(f32) + dt_bias.astype(f32))).transpose(0, 2, 1)
    o = gated_delta_rule(q, k, v, g, beta).transpose(0, 2, 1, 3)
    zf = z.astype(f32).reshape(b, s, DN_HEADS, DN_HEAD_DIM)
    o = rmsnorm(o, dn_norm_w) * jax.nn.silu(zf)
    return o.reshape(b, s, DN_WIDTH).astype(out_dtype)


def peer_ffn(x, w_query, keys_1, keys_2, expert_down, expert_up):
    b, s, d = x.shape
    f32 = jnp.float32
    q = (x @ w_query).astype(f32).reshape(b, s, PEER_HEADS, 2, PEER_HALF)
    s1 = jnp.einsum("bshc,hkc->bshk", q[..., 0, :], keys_1.astype(f32))
    s2 = jnp.einsum("bshc,hkc->bshk", q[..., 1, :], keys_2.astype(f32))
    v1, i1 = lax.top_k(s1, PEER_TOPK)
    v2, i2 = lax.top_k(s2, PEER_TOPK)
    cand_score = (v1[..., :, None] + v2[..., None, :]).reshape(b, s, PEER_HEADS, PEER_TOPK * PEER_TOPK)
    cand_idx = (i1[..., :, None] * N_KEYS + i2[..., None, :]).reshape(b, s, PEER_HEADS, PEER_TOPK * PEER_TOPK)
    top_score, pos = lax.top_k(cand_score, PEER_TOPK)
    experts = jnp.take_along_axis(cand_idx, pos, axis=-1)
    gates = jax.nn.softmax(top_score, axis=-1)

    nb = s // PEER_BLOCK
    xb = x.reshape(b, nb, PEER_BLOCK, d).transpose(1, 0, 2, 3)
    eb = experts.reshape(b, nb, PEER_BLOCK, PEER_HEADS, PEER_TOPK).transpose(1, 0, 2, 3, 4)
    gb = gates.reshape(b, nb, PEER_BLOCK, PEER_HEADS, PEER_TOPK).transpose(1, 0, 2, 3, 4)

    def block(args):
        xi, ei, gi = args
        u = jnp.take(expert_down, ei, axis=0)
        act = jax.nn.gelu(jnp.einsum("btd,bthkd->bthk", xi, u).astype(f32), approximate=False)
        vv = jnp.take(expert_up, ei, axis=0)
        return jnp.einsum("bthk,bthkd->btd", (gi * act).astype(x.dtype), vv)

    y = lax.map(block, (xb, eb, gb))
    return y.transpose(1, 0, 2, 3).reshape(b, s, d)


def setup_inputs(seed: int = 0) -> dict:
    key = jax.random.key(seed)
    ks = jax.random.split(key, 20)
    f32 = jnp.float32

    def nrm(k, shape, scale):
        return scale * jax.random.normal(k, shape, f32)

    def gain(k, shape):
        return 1.0 + 0.05 * jax.random.normal(k, shape, f32)

    dt = jnp.exp(jax.random.uniform(ks[7], (DEPTH, DN_HEADS), f32, math.log(1e-3), math.log(1e-1)))
    dt_bias = dt + jnp.log(-jnp.expm1(-dt))
    a_log = jnp.log(jax.random.uniform(ks[8], (DEPTH, DN_HEADS), f32, 1.0, 16.0))
    return {
        "x": jax.random.normal(ks[0], (BATCH, SEQ, D_MODEL), f32),
        "mix_norm_w": gain(ks[1], (DEPTH, D_MODEL)),
        "w_in": nrm(ks[2], (DEPTH, D_MODEL, IN_COLS), D_MODEL ** -0.5),
        "pool_w": nrm(ks[3], (DEPTH, POOL_GROUPS, POOL_GROUP_DIM, POOL_GROUP_DIM), POOL_GROUP_DIM ** -0.5),
        "pool_scale": 1.0 + 0.1 * jax.random.normal(ks[4], (DEPTH, POOL_WIDTH), f32),
        "conv_w": nrm(ks[5], (DEPTH, CONV_WIDTH, 3 * DN_WIDTH), CONV_WIDTH ** -0.5),
        "a_log": a_log,
        "dt_bias": dt_bias,
        "dn_norm_w": gain(ks[6], (DEPTH, DN_HEAD_DIM)),
        "w_pool_up": nrm(ks[9], (DEPTH, POOL_WIDTH, D_MODEL), POOL_WIDTH ** -0.5),
        "w_dn_up": nrm(ks[10], (DEPTH, DN_WIDTH, D_MODEL), DN_WIDTH ** -0.5),
        "w_mix_out": nrm(ks[11], (DEPTH, D_MODEL, D_MODEL), D_MODEL ** -0.5),
        "ffn_norm_w": gain(ks[12], (DEPTH, D_MODEL)),
        "peer_w_query": nrm(ks[13], (DEPTH, D_MODEL, PEER_HEADS * PEER_QUERY_DIM), D_MODEL ** -0.5),
        "peer_keys_1": nrm(ks[14], (DEPTH, PEER_HEADS, N_KEYS, PEER_HALF), PEER_HALF ** -0.5),
        "peer_keys_2": nrm(ks[15], (DEPTH, PEER_HEADS, N_KEYS, PEER_HALF), PEER_HALF ** -0.5),
        "peer_down": nrm(ks[16], (DEPTH, N_EXPERTS, D_MODEL), D_MODEL ** -0.5),
        "peer_up": nrm(ks[17], (DEPTH, N_EXPERTS, D_MODEL), PEER_HEADS ** -0.5),
        "final_norm_w": gain(ks[18], (D_MODEL,)),
    }


def reference(x, mix_norm_w, w_in, pool_w, pool_scale, conv_w, a_log, dt_bias, dn_norm_w,
              w_pool_up, w_dn_up, w_mix_out, ffn_norm_w, peer_w_query, peer_keys_1,
              peer_keys_2, peer_down, peer_up, final_norm_w):
    h = x
    for l in range(DEPTH):
        xn = rmsnorm(h, mix_norm_w[l])
        proj = xn @ w_in[l]
        xa, qkv, z, beta_logit, a_logit, gate_a, gate_b = jnp.split(proj, SPLIT_POINTS, axis=-1)
        y_a = pool_mixer(xa, pool_w[l], pool_scale[l]) @ w_pool_up[l]
        y_b = deltanet_branch(qkv, z, beta_logit, a_logit, conv_w[l], a_log[l], dt_bias[l],
                              dn_norm_w[l]) @ w_dn_up[l]
        merged = jax.nn.sigmoid(gate_a) * y_a + jax.nn.sigmoid(gate_b) * y_b
        h = h + merged @ w_mix_out[l]
        h = h + peer_ffn(rmsnorm(h, ffn_norm_w[l]), peer_w_query[l], peer_keys_1[l],
                         peer_keys_2[l], peer_down[l], peer_up[l])
    return rmsnorm(h, final_norm_w)
```

```python
import functools

import jax
import jax.numpy as jnp
from jax import lax
from jax.experimental import pallas as pl
from jax.experimental.pallas import tpu as pltpu

F32 = jnp.float32
BF16 = jnp.bfloat16
EPS = 1e-6

D_MODEL = 1024
POOL_WINDOWS = (2, 4, 8, 16)
POOL_GROUP_DIM = 128
POOL_WIDTH = 512
POOL_HALO = 16
DN_HEADS = 8
DN_HEAD_DIM = 128
DN_WIDTH = DN_HEADS * DN_HEAD_DIM
CONV_WIDTH = 4
CONV_HALO = 8
DN_CHUNK = 128
PEER_HEADS = 8
N_KEYS = 128
PEER_TOPK = 16
PEER_HALF = 128

V7X_LANES = 128
VMEM_LIMIT = 56 * 1024 * 1024

IN_TILE = 256
POOL_TILE = 512
POST_TILE = 256
PEER_TILE = 512
PEER_IBLK = 4

PEER_CANDS = tuple((a, b) for a in range(PEER_TOPK) for b in range(PEER_TOPK)
                   if (a + 1) * (b + 1) <= PEER_TOPK)
PEER_CAND_ROWS = 56

_NT = (((1,), (1,)), ((), ()))
_TN = (((0,), (0,)), ((), ()))


def _params(*sem):
    return pltpu.CompilerParams(dimension_semantics=sem, vmem_limit_bytes=VMEM_LIMIT)


def _const_spec(shape):
    return pl.BlockSpec(shape, lambda *_: (0,) * len(shape))


def _sigmoid(x):
    return 1.0 / (1.0 + jnp.exp(-x))


def _in_proj_kernel(x_ref, nw_ref, wxa_ref, wqkv_ref, wz_ref, wga_ref, wgb_ref, wbd_ref,
                    xa_ref, qkv_ref, z_ref, ga_ref, gb_ref, bd_ref):
    x = x_ref[...]
    xn = x * lax.rsqrt(jnp.mean(x * x, axis=-1, keepdims=True) + EPS) * nw_ref[...]
    xb = xn.astype(BF16)
    xa_ref[...] = jnp.dot(xb, wxa_ref[...], preferred_element_type=F32)
    qkv_ref[...] = jnp.dot(xb, wqkv_ref[...], preferred_element_type=F32)
    z_ref[...] = jnp.dot(xb, wz_ref[...], preferred_element_type=F32)
    ga_ref[...] = jnp.dot(xb, wga_ref[...], preferred_element_type=F32)
    gb_ref[...] = jnp.dot(xb, wgb_ref[...], preferred_element_type=F32)
    bd_ref[...] = jnp.dot(xn, wbd_ref[...], preferred_element_type=F32,
                          precision=lax.Precision.HIGHEST)


def _in_proj(x2, nw, wxa, wqkv, wz, wga, wgb, wbd):
    n = x2.shape[0]
    t = IN_TILE
    row = lambda w: pl.BlockSpec((t, w), lambda i: (i, 0))
    widths = (POOL_WIDTH, 3 * DN_WIDTH, DN_WIDTH, D_MODEL, D_MODEL, 2 * DN_HEADS)
    return pl.pallas_call(
        _in_proj_kernel,
        grid=(n // t,),
        in_specs=[row(D_MODEL), _const_spec(nw.shape), _const_spec(wxa.shape), _const_spec(wqkv.shape),
                  _const_spec(wz.shape), _const_spec(wga.shape), _const_spec(wgb.shape),
                  _const_spec(wbd.shape)],
        out_specs=[row(w) for w in widths],
        out_shape=[jax.ShapeDtypeStruct((n, w), F32) for w in widths],
        compiler_params=_params("parallel"),
        name="in_proj",
    )(x2, nw, wxa, wqkv, wz, wga, wgb, wbd)


def _pool_kernel(xa_ref, pw_ref, ps_ref, wup_ref, ya_ref, buf_ref):
    s = pl.program_id(1)
    t = xa_ref.shape[1]

    @pl.when(s == 0)
    def _():
        buf_ref[0:POOL_HALO, :] = jnp.zeros((POOL_HALO, POOL_WIDTH), F32)

    @pl.when(s > 0)
    def _():
        buf_ref[0:POOL_HALO, :] = buf_ref[t:t + POOL_HALO, :]

    buf_ref[POOL_HALO:, :] = xa_ref[0]
    pos = s * t + lax.broadcasted_iota(jnp.int32, (t, 1), 0)
    ys = []
    for g, win in enumerate(POOL_WINDOWS):
        sl = slice(g * POOL_GROUP_DIM, (g + 1) * POOL_GROUP_DIM)
        xg = buf_ref[POOL_HALO:POOL_HALO + t, sl]
        acc = xg
        for k in range(1, win):
            acc = acc + buf_ref[POOL_HALO - k:POOL_HALO - k + t, sl]
        cnt = jnp.minimum(pos + 1, win).astype(F32)
        pooled = acc / cnt - xg
        y = jnp.dot(pooled.astype(BF16), pw_ref[g], preferred_element_type=F32)
        ys.append(y * ps_ref[:, sl])
    y = jnp.concatenate(ys, axis=-1)
    ya_ref[0] = jnp.dot(y.astype(BF16), wup_ref[...], preferred_element_type=F32)


def _pool(xa, pw, ps, wup):
    b, s, _ = xa.shape
    t = POOL_TILE
    return pl.pallas_call(
        _pool_kernel,
        grid=(b, s // t),
        in_specs=[pl.BlockSpec((1, t, POOL_WIDTH), lambda i, j: (i, j, 0)),
                  _const_spec(pw.shape), _const_spec(ps.shape), _const_spec(wup.shape)],
        out_specs=pl.BlockSpec((1, t, D_MODEL), lambda i, j: (i, j, 0)),
        out_shape=jax.ShapeDtypeStruct((b, s, D_MODEL), F32),
        scratch_shapes=[pltpu.VMEM((t + POOL_HALO, POOL_WIDTH), F32)],
        compiler_params=_params("parallel", "arbitrary"),
        name="pool",
    )(xa, pw, ps, wup)


def _softplus(x):
    return jnp.maximum(x, 0.0) + jnp.log1p(jnp.exp(-jnp.abs(x)))


def _dn_kernel(qkv_ref, bd_ref, bdt_ref, cw_ref, alog_ref, dtb_ref, alogt_ref, dtbt_ref,
               o_ref, cbuf_ref, st_ref):
    s = pl.program_id(1)
    c = DN_CHUNK
    hd = DN_HEAD_DIM

    @pl.when(s == 0)
    def _():
        cbuf_ref[0:CONV_HALO, :] = jnp.zeros((CONV_HALO, 3 * DN_WIDTH), F32)
        st_ref[...] = jnp.zeros(st_ref.shape, F32)

    @pl.when(s > 0)
    def _():
        cbuf_ref[0:CONV_HALO, :] = cbuf_ref[c:c + CONV_HALO, :]

    cbuf_ref[CONV_HALO:, :] = qkv_ref[0]

    bd = bd_ref[0]
    bdt = bdt_ref[0]
    beta_all = _sigmoid(bd[:, 0:DN_HEADS])
    g_all = -jnp.exp(alog_ref[...]) * _softplus(bd[:, DN_HEADS:] + dtb_ref[...])
    g_allt = -jnp.exp(alogt_ref[...]) * _softplus(bdt[DN_HEADS:, :] + dtbt_ref[...])

    row = lax.broadcasted_iota(jnp.int32, (c, c), 0)
    col = lax.broadcasted_iota(jnp.int32, (c, c), 1)
    tril = row >= col
    strict = row > col
    gc = jnp.dot(tril.astype(F32), g_all, preferred_element_type=F32, precision=lax.Precision.HIGHEST)
    gct = jnp.dot(g_allt, (row <= col).astype(F32), preferred_element_type=F32,
                  precision=lax.Precision.HIGHEST)
    eye = (row == col).astype(F32)

    def conv_silu(lane0):
        sl = slice(lane0, lane0 + hd)
        y = cw_ref[0:1, sl] * cbuf_ref[CONV_HALO - 3:CONV_HALO - 3 + c, sl]
        for j in range(1, CONV_WIDTH):
            y = y + cw_ref[j:j + 1, sl] * cbuf_ref[CONV_HALO - 3 + j:CONV_HALO - 3 + j + c, sl]
        return y * _sigmoid(y)

    def l2n(v):
        return v * lax.rsqrt(jnp.sum(v * v, axis=-1, keepdims=True) + EPS)

    for h in range(DN_HEADS):
        q = l2n(conv_silu(h * hd)) * (hd ** -0.5)
        k = l2n(conv_silu(DN_WIDTH + h * hd))
        v = conv_silu(2 * DN_WIDTH + h * hd)
        beta = beta_all[:, h:h + 1]
        gcol = gc[:, h:h + 1]
        grow = gct[h:h + 1, :]
        decay = jnp.where(tril, jnp.exp(jnp.where(tril, gcol - grow, 0.0)), 0.0)
        kb = k * beta
        vb = v * beta
        kf = k.astype(BF16)
        a = lax.dot_general(kb.astype(BF16), kf, _NT, preferred_element_type=F32)
        lower = jnp.where(strict, a * decay, 0.0)
        attn = jnp.where(tril, lax.dot_general(q.astype(BF16), kf, _NT, preferred_element_type=F32) * decay, 0.0)
        tinv = eye - jnp.where(((row ^ col) == 1) & strict, lower, 0.0)
        blk = 2
        while blk < c:
            m = ((row ^ col) < 2 * blk) & ((row & blk) != 0) & ((col & blk) == 0)
            lo = jnp.where(m, lower, 0.0).astype(BF16)
            tb = tinv.astype(BF16)
            left = jnp.dot(tb, lo, preferred_element_type=F32).astype(BF16)
            tinv = tinv - jnp.dot(left, tb, preferred_element_type=F32)
            blk *= 2
        tb = tinv.astype(BF16)
        u = jnp.dot(tb, vb.astype(BF16), preferred_element_type=F32)
        eg = jnp.exp(gcol)
        w = jnp.dot(tb, (kb * eg).astype(BF16), preferred_element_type=F32)

        st = st_ref[h]
        sb = st.astype(BF16)
        v_new = u - jnp.dot(w.astype(BF16), sb, preferred_element_type=F32)
        vnb = v_new.astype(BF16)
        o = (jnp.dot((q * eg).astype(BF16), sb, preferred_element_type=F32)
             + jnp.dot(attn.astype(BF16), vnb, preferred_element_type=F32))
        g_last = grow[:, c - 1:c]
        k_dec = k * jnp.exp(g_last - gcol)
        st_ref[h] = st * jnp.exp(g_last) + lax.dot_general(k_dec.astype(BF16), vnb, _TN,
                                                           preferred_element_type=F32)
        o_ref[0, :, h * hd:(h + 1) * hd] = o


def _deltanet(qkv, bd, bdt, cw, alog, dtb):
    b, s, _ = qkv.shape
    c = DN_CHUNK
    alog2, dtb2 = alog.reshape(1, DN_HEADS), dtb.reshape(1, DN_HEADS)
    alogt, dtbt = alog.reshape(DN_HEADS, 1), dtb.reshape(DN_HEADS, 1)
    return pl.pallas_call(
        _dn_kernel,
        grid=(b, s // c),
        in_specs=[pl.BlockSpec((1, c, 3 * DN_WIDTH), lambda i, j: (i, j, 0)),
                  pl.BlockSpec((1, c, 2 * DN_HEADS), lambda i, j: (i, j, 0)),
                  pl.BlockSpec((1, 2 * DN_HEADS, c), lambda i, j: (i, 0, j)),
                  _const_spec(cw.shape), _const_spec(alog2.shape), _const_spec(dtb2.shape),
                  _const_spec(alogt.shape), _const_spec(dtbt.shape)],
        out_specs=pl.BlockSpec((1, c, DN_WIDTH), lambda i, j: (i, j, 0)),
        out_shape=jax.ShapeDtypeStruct((b, s, DN_WIDTH), F32),
        scratch_shapes=[pltpu.VMEM((c + CONV_HALO, 3 * DN_WIDTH), F32),
                        pltpu.VMEM((DN_HEADS, DN_HEAD_DIM, DN_HEAD_DIM), F32)],
        compiler_params=_params("parallel", "arbitrary"),
        name="deltanet",
    )(qkv, bd, bdt, cw, alog2, dtb2, alogt, dtbt)


def _rms(x, w):
    return x * lax.rsqrt(jnp.mean(x * x, axis=-1, keepdims=True) + EPS) * w


def _post_kernel(o_ref, z_ref, ga_ref, gb_ref, ya_ref, x_ref, dnw_ref, wdn_ref, wmix_ref, fnw_ref,
                 h1_ref, xn2_ref):
    o = o_ref[...]
    z = z_ref[...]
    parts = []
    for h in range(DN_HEADS):
        sl = slice(h * DN_HEAD_DIM, (h + 1) * DN_HEAD_DIM)
        parts.append(_rms(o[:, sl], dnw_ref[...]))
    on = jnp.concatenate(parts, axis=-1) * (z * _sigmoid(z))
    yb = jnp.dot(on.astype(BF16), wdn_ref[...], preferred_element_type=F32)
    merged = _sigmoid(ga_ref[...]) * ya_ref[...] + _sigmoid(gb_ref[...]) * yb
    h1 = x_ref[...] + jnp.dot(merged.astype(BF16), wmix_ref[...], preferred_element_type=F32)
    h1_ref[...] = h1
    xn2_ref[...] = _rms(h1, fnw_ref[...]).astype(BF16)


def _post(o, z, ga, gb, ya, x2, dnw, wdn, wmix, fnw):
    n = x2.shape[0]
    t = POST_TILE
    row = pl.BlockSpec((t, D_MODEL), lambda i: (i, 0))
    return pl.pallas_call(
        _post_kernel,
        grid=(n // t,),
        in_specs=[row, row, row, row, row, row, _const_spec(dnw.shape), _const_spec(wdn.shape),
                  _const_spec(wmix.shape), _const_spec(fnw.shape)],
        out_specs=[row, row],
        out_shape=[jax.ShapeDtypeStruct((n, D_MODEL), F32), jax.ShapeDtypeStruct((n, D_MODEL), BF16)],
        compiler_params=_params("parallel"),
        name="post",
    )(o, z, ga, gb, ya, x2, dnw, wdn, wmix, fnw)


def _extract_top(work, rounds, vals_ref=None):
    rows = work.shape[0]
    iota = lax.broadcasted_iota(jnp.int32, work.shape, 0).astype(F32)
    rank = jnp.full(work.shape, float(rounds), F32)
    for r in range(rounds):
        m = jnp.max(work, axis=0, keepdims=True)
        idx = jnp.min(jnp.where(work == m, iota, float(rows)), axis=0, keepdims=True)
        hit = iota == idx
        rank = jnp.where(hit, float(r), rank)
        work = jnp.where(hit, -jnp.inf, work)
        if vals_ref is not None:
            vals_ref[r:r + 1, :] = m
    return rank


def _peer_kernel(xn_ref, h1_ref, wq_ref, k1_ref, k2_ref, u_ref, vt_ref, fw_ref, out_ref,
                 e1_ref, n1_ref, r2_ref, e2_ref, v1_ref, v2_ref, cand_ref, yt_ref):
    ib = pl.program_id(1)
    nib = pl.num_programs(1)
    t = xn_ref.shape[0]
    xn = xn_ref[...]

    @pl.when(ib == 0)
    def _():
        yt_ref[...] = jnp.zeros(yt_ref.shape, F32)

        def head(h, carry):
            r0 = pl.multiple_of(h * 2 * PEER_HALF, 2 * PEER_HALF)
            q1 = lax.dot_general(wq_ref[pl.ds(r0, PEER_HALF), :], xn, _NT, preferred_element_type=F32)
            q2 = lax.dot_general(wq_ref[pl.ds(r0 + PEER_HALF, PEER_HALF), :], xn, _NT,
                                 preferred_element_type=F32)
            s1 = jnp.dot(k1_ref[h], q1.astype(BF16), preferred_element_type=F32)
            s2 = jnp.dot(k2_ref[h], q2.astype(BF16), preferred_element_type=F32)
            rank1 = _extract_top(s1, PEER_TOPK, v1_ref)
            rank2 = _extract_top(s2, PEER_TOPK, v2_ref)
            cand_ref[...] = jnp.full(cand_ref.shape, -jnp.inf, F32)
            for n, (a, b) in enumerate(PEER_CANDS):
                cand_ref[n:n + 1, :] = v1_ref[a:a + 1, :] + v2_ref[b:b + 1, :]
            cand = cand_ref[...]
            sel = _extract_top(cand, PEER_TOPK) < float(PEER_TOPK)
            cexp = jnp.where(sel, jnp.exp(jnp.where(sel, cand - cand[0:1, :], 0.0)), 0.0)
            inv_z = 1.0 / jnp.sum(cexp, axis=0, keepdims=True)
            self32 = sel.astype(F32)
            n1 = jnp.zeros(rank1.shape, F32)
            n = 0
            for a in range(PEER_TOPK):
                width = PEER_TOPK // (a + 1)
                n_a = jnp.sum(self32[n:n + width, :], axis=0, keepdims=True)
                n1 = jnp.where(rank1 == float(a), n_a, n1)
                n += width
            in1 = rank1 < float(PEER_TOPK)
            in2 = rank2 < float(PEER_TOPK)
            e1 = jnp.where(in1, jnp.exp(jnp.where(in1, s1 - v1_ref[0:1, :], 0.0)) * inv_z, 0.0)
            e2 = jnp.where(in2, jnp.exp(jnp.where(in2, s2 - v2_ref[0:1, :], 0.0)), 0.0)
            e1_ref[h] = e1
            n1_ref[h] = n1
            r2_ref[h] = rank2
            e2_ref[h] = e2
            return carry

        lax.fori_loop(0, PEER_HEADS, head, 0)

    ht = lax.dot_general(u_ref[...], xn, _NT, preferred_element_type=F32)
    ps = []
    for kk in range(PEER_IBLK):
        i = ib * PEER_IBLK + kk
        hk = ht[kk * N_KEYS:(kk + 1) * N_KEYS, :]
        act = 0.5 * hk * (1.0 + lax.erf(hk * 0.7071067811865476))
        g = jnp.zeros((N_KEYS, t), F32)
        for h in range(PEER_HEADS):
            e1row = e1_ref[h, pl.ds(i, 1), :]
            n1row = n1_ref[h, pl.ds(i, 1), :]
            g = g + e1row * jnp.where(r2_ref[h] < n1row, e2_ref[h], 0.0)
        ps.append((g * act).astype(BF16))
    p = jnp.concatenate(ps, axis=0)
    yt_ref[...] += jnp.dot(vt_ref[...], p, preferred_element_type=F32)

    @pl.when(ib == nib - 1)
    def _():
        hfin = h1_ref[...] + yt_ref[...].T
        out_ref[...] = _rms(hfin, fw_ref[...])


def _peer(xn2, h1, wqt, k1, k2, u, vt, fw):
    n = xn2.shape[0]
    t = PEER_TILE
    eb = PEER_IBLK * N_KEYS
    nib = (N_KEYS * N_KEYS) // eb
    tok = pl.BlockSpec((t, D_MODEL), lambda i, j: (i, 0))
    big = pltpu.VMEM((PEER_HEADS, N_KEYS, t), F32)
    return pl.pallas_call(
        _peer_kernel,
        grid=(n // t, nib),
        in_specs=[tok, tok, _const_spec(wqt.shape), _const_spec(k1.shape), _const_spec(k2.shape),
                  pl.BlockSpec((eb, D_MODEL), lambda i, j: (j, 0)),
                  pl.BlockSpec((D_MODEL, eb), lambda i, j: (0, j)),
                  _const_spec(fw.shape)],
        out_specs=tok,
        out_shape=jax.ShapeDtypeStruct((n, D_MODEL), F32),
        scratch_shapes=[big, big, big, big,
                        pltpu.VMEM((PEER_TOPK, t), F32), pltpu.VMEM((PEER_TOPK, t), F32),
                        pltpu.VMEM((PEER_CAND_ROWS, t), F32),
                        pltpu.VMEM((D_MODEL, t), F32)],
        compiler_params=_params("parallel", "arbitrary"),
        name="peer",
    )(xn2, h1, wqt, k1, k2, u, vt, fw)


def kernel(x, mix_norm_w, w_in, pool_w, pool_scale, conv_w, a_log, dt_bias, dn_norm_w, w_pool_up,
           w_dn_up, w_mix_out, ffn_norm_w, peer_w_query, peer_keys_1, peer_keys_2, peer_down, peer_up,
           final_norm_w):
    b, s, d = x.shape
    n = b * s
    assert d == D_MODEL and s % POOL_TILE == 0 and s % DN_CHUNK == 0
    assert n % IN_TILE == 0 and n % POST_TILE == 0 and n % PEER_TILE == 0
    assert w_in.shape[0] == 1, "single-layer block"
    l = 0
    h = x.reshape(n, d)

    c0 = POOL_WIDTH
    c1 = c0 + 3 * DN_WIDTH
    c2 = c1 + DN_WIDTH
    c3 = c2 + 2 * DN_HEADS
    c4 = c3 + D_MODEL
    wi = w_in[l]
    xa, qkv, z, ga, gb, bd = _in_proj(
        h, mix_norm_w[l].reshape(1, d),
        wi[:, :c0].astype(BF16), wi[:, c0:c1].astype(BF16), wi[:, c1:c2].astype(BF16),
        wi[:, c3:c4].astype(BF16), wi[:, c4:].astype(BF16), wi[:, c2:c3])

    ya = _pool(xa.reshape(b, s, POOL_WIDTH), pool_w[l].astype(BF16), pool_scale[l].reshape(1, POOL_WIDTH),
               w_pool_up[l].astype(BF16))

    bd3 = bd.reshape(b, s, 2 * DN_HEADS)
    o = _deltanet(qkv.reshape(b, s, 3 * DN_WIDTH), bd3, bd3.transpose(0, 2, 1), conv_w[l], a_log[l],
                  dt_bias[l])

    h1, xn2 = _post(o.reshape(n, DN_WIDTH), z, ga, gb, ya.reshape(n, d), h,
                    dn_norm_w[l].reshape(1, DN_HEAD_DIM), w_dn_up[l].astype(BF16),
                    w_mix_out[l].astype(BF16), ffn_norm_w[l].reshape(1, d))

    out = _peer(xn2, h1, peer_w_query[l].T.astype(BF16), peer_keys_1[l].astype(BF16),
                peer_keys_2[l].astype(BF16), peer_down[l].astype(BF16), peer_up[l].T.astype(BF16),
                final_norm_w.reshape(1, d))
    return out.reshape(b, s, d)
```

```python
import functools

import jax
import jax.numpy as jnp
from jax import lax
from jax.experimental import pallas as pl
from jax.experimental.pallas import tpu as pltpu

F32 = jnp.float32
BF16 = jnp.bfloat16
EPS = 1e-6

D_MODEL = 1024
POOL_WINDOWS = (2, 4, 8, 16)
POOL_GROUP_DIM = 128
POOL_WIDTH = 512
POOL_HALO = 16
DN_HEADS = 8
DN_HEAD_DIM = 128
DN_WIDTH = DN_HEADS * DN_HEAD_DIM
CONV_WIDTH = 4
CONV_HALO = 8
DN_CHUNK = 128
PEER_HEADS = 8
N_KEYS = 128
PEER_TOPK = 16
PEER_HALF = 128

V7X_LANES = 128
VMEM_LIMIT = 56 * 1024 * 1024

IN_TILE = 256
POOL_TILE = 512
POST_TILE = 256
PEER_TILE = 512
PEER_IBLK = 4

PEER_CANDS = tuple((a, b) for a in range(PEER_TOPK) for b in range(PEER_TOPK)
                   if (a + 1) * (b + 1) <= PEER_TOPK)
PEER_CAND_ROWS = 56

_NT = (((1,), (1,)), ((), ()))
_TN = (((0,), (0,)), ((), ()))


def _params(*sem):
    return pltpu.CompilerParams(dimension_semantics=sem, vmem_limit_bytes=VMEM_LIMIT)


def _const_spec(shape):
    return pl.BlockSpec(shape, lambda *_: (0,) * len(shape))


def _sigmoid(x):
    return 1.0 / (1.0 + jnp.exp(-x))


def _in_proj_kernel(x_ref, nw_ref, wxa_ref, wqkv_ref, wz_ref, wga_ref, wgb_ref, wbd_ref,
                    xa_ref, qkv_ref, z_ref, ga_ref, gb_ref, bd_ref):
    x = x_ref[...]
    xn = x * lax.rsqrt(jnp.mean(x * x, axis=-1, keepdims=True) + EPS) * nw_ref[...]
    xb = xn.astype(BF16)
    xa_ref[...] = jnp.dot(xb, wxa_ref[...], preferred_element_type=F32)
    qkv_ref[...] = jnp.dot(xb, wqkv_ref[...], preferred_element_type=F32)
    z_ref[...] = jnp.dot(xb, wz_ref[...], preferred_element_type=F32)
    ga_ref[...] = jnp.dot(xb, wga_ref[...], preferred_element_type=F32)
    gb_ref[...] = jnp.dot(xb, wgb_ref[...], preferred_element_type=F32)
    bd_ref[...] = jnp.dot(xn, wbd_ref[...], preferred_element_type=F32,
                          precision=lax.Precision.HIGHEST)


def _in_proj(x2, nw, wxa, wqkv, wz, wga, wgb, wbd):
    n = x2.shape[0]
    t = IN_TILE
    row = lambda w: pl.BlockSpec((t, w), lambda i: (i, 0))
    widths = (POOL_WIDTH, 3 * DN_WIDTH, DN_WIDTH, D_MODEL, D_MODEL, 2 * DN_HEADS)
    return pl.pallas_call(
        _in_proj_kernel,
        grid=(n // t,),
        in_specs=[row(D_MODEL), _const_spec(nw.shape), _const_spec(wxa.shape), _const_spec(wqkv.shape),
                  _const_spec(wz.shape), _const_spec(wga.shape), _const_spec(wgb.shape),
                  _const_spec(wbd.shape)],
        out_specs=[row(w) for w in widths],
        out_shape=[jax.ShapeDtypeStruct((n, w), F32) for w in widths],
        compiler_params=_params("parallel"),
        name="in_proj",
    )(x2, nw, wxa, wqkv, wz, wga, wgb, wbd)


def _pool_kernel(xa_ref, pw_ref, ps_ref, wup_ref, ya_ref, buf_ref):
    s = pl.program_id(1)
    t = xa_ref.shape[1]

    @pl.when(s == 0)
    def _():
        buf_ref[0:POOL_HALO, :] = jnp.zeros((POOL_HALO, POOL_WIDTH), F32)

    @pl.when(s > 0)
    def _():
        buf_ref[0:POOL_HALO, :] = buf_ref[t:t + POOL_HALO, :]

    buf_ref[POOL_HALO:, :] = xa_ref[0]
    pos = s * t + lax.broadcasted_iota(jnp.int32, (t, 1), 0)
    ys = []
    for g, win in enumerate(POOL_WINDOWS):
        sl = slice(g * POOL_GROUP_DIM, (g + 1) * POOL_GROUP_DIM)
        xg = buf_ref[POOL_HALO:POOL_HALO + t, sl]
        acc = xg
        for k in range(1, win):
            acc = acc + buf_ref[POOL_HALO - k:POOL_HALO - k + t, sl]
        cnt = jnp.minimum(pos + 1, win).astype(F32)
        pooled = acc / cnt - xg
        y = jnp.dot(pooled.astype(BF16), pw_ref[g], preferred_element_type=F32)
        ys.append(y * ps_ref[:, sl])
    y = jnp.concatenate(ys, axis=-1)
    ya_ref[0] = jnp.dot(y.astype(BF16), wup_ref[...], preferred_element_type=F32)


def _pool(xa, pw, ps, wup):
    b, s, _ = xa.shape
    t = POOL_TILE
    return pl.pallas_call(
        _pool_kernel,
        grid=(b, s // t),
        in_specs=[pl.BlockSpec((1, t, POOL_WIDTH), lambda i, j: (i, j, 0)),
                  _const_spec(pw.shape), _const_spec(ps.shape), _const_spec(wup.shape)],
        out_specs=pl.BlockSpec((1, t, D_MODEL), lambda i, j: (i, j, 0)),
        out_shape=jax.ShapeDtypeStruct((b, s, D_MODEL), F32),
        scratch_shapes=[pltpu.VMEM((t + POOL_HALO, POOL_WIDTH), F32)],
        compiler_params=_params("parallel", "arbitrary"),
        name="pool",
    )(xa, pw, ps, wup)


def _softplus(x):
    return jnp.maximum(x, 0.0) + jnp.log1p(jnp.exp(-jnp.abs(x)))


def _dn_kernel(qkv_ref, bd_ref, bdt_ref, cw_ref, alog_ref, dtb_ref, alogt_ref, dtbt_ref,
               o_ref, cbuf_ref, st_ref):
    s = pl.program_id(1)
    c = DN_CHUNK
    hd = DN_HEAD_DIM

    @pl.when(s == 0)
    def _():
        cbuf_ref[0:CONV_HALO, :] = jnp.zeros((CONV_HALO, 3 * DN_WIDTH), F32)
        st_ref[...] = jnp.zeros(st_ref.shape, F32)

    @pl.when(s > 0)
    def _():
        cbuf_ref[0:CONV_HALO, :] = cbuf_ref[c:c + CONV_HALO, :]

    cbuf_ref[CONV_HALO:, :] = qkv_ref[0]

    bd = bd_ref[0]
    bdt = bdt_ref[0]
    beta_all = _sigmoid(bd[:, 0:DN_HEADS])
    g_all = -jnp.exp(alog_ref[...]) * _softplus(bd[:, DN_HEADS:] + dtb_ref[...])
    g_allt = -jnp.exp(alogt_ref[...]) * _softplus(bdt[DN_HEADS:, :] + dtbt_ref[...])

    row = lax.broadcasted_iota(jnp.int32, (c, c), 0)
    col = lax.broadcasted_iota(jnp.int32, (c, c), 1)
    tril = row >= col
    strict = row > col
    gc = jnp.dot(tril.astype(F32), g_all, preferred_element_type=F32, precision=lax.Precision.HIGHEST)
    gct = jnp.dot(g_allt, (row <= col).astype(F32), preferred_element_type=F32,
                  precision=lax.Precision.HIGHEST)
    eye = (row == col).astype(F32)

    def conv_silu(lane0):
        sl = slice(lane0, lane0 + hd)
        y = cw_ref[0:1, sl] * cbuf_ref[CONV_HALO - 3:CONV_HALO - 3 + c, sl]
        for j in range(1, CONV_WIDTH):
            y = y + cw_ref[j:j + 1, sl] * cbuf_ref[CONV_HALO - 3 + j:CONV_HALO - 3 + j + c, sl]
        return y * _sigmoid(y)

    def l2n(v):
        return v * lax.rsqrt(jnp.sum(v * v, axis=-1, keepdims=True) + EPS)

    hs = range(DN_HEADS)
    q = [l2n(conv_silu(h * hd)) * (hd ** -0.5) for h in hs]
    k = [l2n(conv_silu(DN_WIDTH + h * hd)) for h in hs]
    v = [conv_silu(2 * DN_WIDTH + h * hd) for h in hs]
    gcol = [gc[:, h:h + 1] for h in hs]
    grow = [gct[h:h + 1, :] for h in hs]
    decay = [jnp.where(tril, jnp.exp(jnp.where(tril, gcol[h] - grow[h], 0.0)), 0.0) for h in hs]
    kb = [k[h] * beta_all[:, h:h + 1] for h in hs]
    vb = [v[h] * beta_all[:, h:h + 1] for h in hs]
    kf = [k[h].astype(BF16) for h in hs]
    lower = [jnp.where(strict, lax.dot_general(kb[h].astype(BF16), kf[h], _NT, preferred_element_type=F32)
                       * decay[h], 0.0) for h in hs]
    attn = [(lax.dot_general(q[h].astype(BF16), kf[h], _NT, preferred_element_type=F32) * decay[h]).astype(BF16)
            for h in hs]
    pair = ((row ^ col) == 1) & strict
    tinv = [eye - jnp.where(pair, lower[h], 0.0) for h in hs]
    blk = 2
    while blk < c:
        m = ((row ^ col) < 2 * blk) & ((row & blk) != 0) & ((col & blk) == 0)
        tb = [tinv[h].astype(BF16) for h in hs]
        left = [jnp.dot(tb[h], jnp.where(m, lower[h], 0.0).astype(BF16), preferred_element_type=F32).astype(BF16)
                for h in hs]
        tinv = [tinv[h] - jnp.dot(left[h], tb[h], preferred_element_type=F32) for h in hs]
        blk *= 2
    eg = [jnp.exp(gcol[h]) for h in hs]
    uw = [jnp.dot(tinv[h].astype(BF16),
                  jnp.concatenate([vb[h], kb[h] * eg[h]], axis=-1).astype(BF16), preferred_element_type=F32)
          for h in hs]
    st = [st_ref[h] for h in hs]
    ws = [jnp.dot(jnp.concatenate([uw[h][:, hd:], q[h] * eg[h]], axis=0).astype(BF16), st[h].astype(BF16),
                  preferred_element_type=F32) for h in hs]
    vnb = [(uw[h][:, :hd] - ws[h][:c]).astype(BF16) for h in hs]
    o = [ws[h][c:] + jnp.dot(attn[h], vnb[h], preferred_element_type=F32) for h in hs]
    for h in hs:
        g_last = grow[h][:, c - 1:c]
        k_dec = (k[h] * jnp.exp(g_last - gcol[h])).astype(BF16)
        st_ref[h] = st[h] * jnp.exp(g_last) + lax.dot_general(k_dec, vnb[h], _TN, preferred_element_type=F32)
    o_ref[0] = jnp.concatenate(o, axis=-1)


def _deltanet(qkv, bd, bdt, cw, alog, dtb):
    b, s, _ = qkv.shape
    c = DN_CHUNK
    alog2, dtb2 = alog.reshape(1, DN_HEADS), dtb.reshape(1, DN_HEADS)
    alogt, dtbt = alog.reshape(DN_HEADS, 1), dtb.reshape(DN_HEADS, 1)
    return pl.pallas_call(
        _dn_kernel,
        grid=(b, s // c),
        in_specs=[pl.BlockSpec((1, c, 3 * DN_WIDTH), lambda i, j: (i, j, 0)),
                  pl.BlockSpec((1, c, 2 * DN_HEADS), lambda i, j: (i, j, 0)),
                  pl.BlockSpec((1, 2 * DN_HEADS, c), lambda i, j: (i, 0, j)),
                  _const_spec(cw.shape), _const_spec(alog2.shape), _const_spec(dtb2.shape),
                  _const_spec(alogt.shape), _const_spec(dtbt.shape)],
        out_specs=pl.BlockSpec((1, c, DN_WIDTH), lambda i, j: (i, j, 0)),
        out_shape=jax.ShapeDtypeStruct((b, s, DN_WIDTH), F32),
        scratch_shapes=[pltpu.VMEM((c + CONV_HALO, 3 * DN_WIDTH), F32),
                        pltpu.VMEM((DN_HEADS, DN_HEAD_DIM, DN_HEAD_DIM), F32)],
        compiler_params=_params("parallel", "arbitrary"),
        name="deltanet",
    )(qkv, bd, bdt, cw, alog2, dtb2, alogt, dtbt)


def _rms(x, w):
    return x * lax.rsqrt(jnp.mean(x * x, axis=-1, keepdims=True) + EPS) * w


def _post_kernel(o_ref, z_ref, ga_ref, gb_ref, ya_ref, x_ref, dnw_ref, wdn_ref, wmix_ref, fnw_ref,
                 h1_ref, xn2_ref):
    o = o_ref[...]
    z = z_ref[...]
    parts = []
    for h in range(DN_HEADS):
        sl = slice(h * DN_HEAD_DIM, (h + 1) * DN_HEAD_DIM)
        parts.append(_rms(o[:, sl], dnw_ref[...]))
    on = jnp.concatenate(parts, axis=-1) * (z * _sigmoid(z))
    yb = jnp.dot(on.astype(BF16), wdn_ref[...], preferred_element_type=F32)
    merged = _sigmoid(ga_ref[...]) * ya_ref[...] + _sigmoid(gb_ref[...]) * yb
    h1 = x_ref[...] + jnp.dot(merged.astype(BF16), wmix_ref[...], preferred_element_type=F32)
    h1_ref[...] = h1
    xn2_ref[...] = _rms(h1, fnw_ref[...]).astype(BF16)


def _post(o, z, ga, gb, ya, x2, dnw, wdn, wmix, fnw):
    n = x2.shape[0]
    t = POST_TILE
    row = pl.BlockSpec((t, D_MODEL), lambda i: (i, 0))
    return pl.pallas_call(
        _post_kernel,
        grid=(n // t,),
        in_specs=[row, row, row, row, row, row, _const_spec(dnw.shape), _const_spec(wdn.shape),
                  _const_spec(wmix.shape), _const_spec(fnw.shape)],
        out_specs=[row, row],
        out_shape=[jax.ShapeDtypeStruct((n, D_MODEL), F32), jax.ShapeDtypeStruct((n, D_MODEL), BF16)],
        compiler_params=_params("parallel"),
        name="post",
    )(o, z, ga, gb, ya, x2, dnw, wdn, wmix, fnw)


_INT_MIN = -2 ** 31


def _order_key(x):
    b = lax.bitcast_convert_type(x, jnp.int32)
    return b ^ ((b >> 31) & 0x7FFFFFFF)


def _key_value(key):
    return lax.bitcast_convert_type(key ^ ((key >> 31) & 0x7FFFFFFF), F32)


def _extract_fast(x, rounds, vals_ref=None):
    key = _order_key(x)
    for r in range(rounds):
        m = jnp.max(key, axis=0, keepdims=True)
        if vals_ref is not None:
            vals_ref[r:r + 1, :] = _key_value(m)
        key = jnp.where(key == m, _INT_MIN + r, key)
    hit = key < _INT_MIN + rounds
    rank = jnp.where(hit, key - _INT_MIN, rounds).astype(F32)
    count = jnp.sum(hit.astype(F32), axis=0, keepdims=True)
    ok = jnp.max(jnp.abs(count - float(rounds))) == 0.0
    return rank, ok


def _extract_exact(work, rounds, vals_ref=None):
    rows = work.shape[0]
    iota = lax.broadcasted_iota(jnp.int32, work.shape, 0).astype(F32)
    rank = jnp.full(work.shape, float(rounds), F32)
    for r in range(rounds):
        m = jnp.max(work, axis=0, keepdims=True)
        idx = jnp.min(jnp.where(work == m, iota, float(rows)), axis=0, keepdims=True)
        hit = iota == idx
        rank = jnp.where(hit, float(r), rank)
        work = jnp.where(hit, -jnp.inf, work)
        if vals_ref is not None:
            vals_ref[r:r + 1, :] = m
    return rank


def _peer_kernel(xn_ref, h1_ref, wq_ref, k1_ref, k2_ref, u_ref, vt_ref, fw_ref, out_ref,
                 e1_ref, n1_ref, r2_ref, e2_ref, rk1_ref, rk2_ref, v1_ref, v2_ref, cand_ref, sel_ref, yt_ref):
    ib = pl.program_id(1)
    nib = pl.num_programs(1)
    t = xn_ref.shape[0]
    xn = xn_ref[...]

    @pl.when(ib == 0)
    def _():
        yt_ref[...] = jnp.zeros(yt_ref.shape, F32)

        def head(h, carry):
            r0 = pl.multiple_of(h * 2 * PEER_HALF, 2 * PEER_HALF)
            q1 = lax.dot_general(wq_ref[pl.ds(r0, PEER_HALF), :], xn, _NT, preferred_element_type=F32)
            q2 = lax.dot_general(wq_ref[pl.ds(r0 + PEER_HALF, PEER_HALF), :], xn, _NT,
                                 preferred_element_type=F32)
            s1 = jnp.dot(k1_ref[h], q1.astype(BF16), preferred_element_type=F32)
            s2 = jnp.dot(k2_ref[h], q2.astype(BF16), preferred_element_type=F32)
            rank1, ok1 = _extract_fast(s1, PEER_TOPK, v1_ref)
            rank2, ok2 = _extract_fast(s2, PEER_TOPK, v2_ref)
            rk1_ref[...] = rank1
            rk2_ref[...] = rank2

            @pl.when(jnp.logical_not(ok1 & ok2))
            def _():
                rk1_ref[...] = _extract_exact(s1, PEER_TOPK, v1_ref)
                rk2_ref[...] = _extract_exact(s2, PEER_TOPK, v2_ref)

            cand_ref[...] = jnp.full(cand_ref.shape, -jnp.inf, F32)
            for n, (a, b) in enumerate(PEER_CANDS):
                cand_ref[n:n + 1, :] = v1_ref[a:a + 1, :] + v2_ref[b:b + 1, :]
            cand = cand_ref[...]
            crank, ok3 = _extract_fast(cand, PEER_TOPK)
            sel_ref[...] = crank

            @pl.when(jnp.logical_not(ok3))
            def _():
                sel_ref[...] = _extract_exact(cand, PEER_TOPK)

            sel = sel_ref[...] < float(PEER_TOPK)
            cexp = jnp.where(sel, jnp.exp(jnp.where(sel, cand - cand[0:1, :], 0.0)), 0.0)
            inv_z = 1.0 / jnp.sum(cexp, axis=0, keepdims=True)
            self32 = sel.astype(F32)
            rank1 = rk1_ref[...]
            n1 = jnp.zeros(rank1.shape, F32)
            n = 0
            for a in range(PEER_TOPK):
                width = PEER_TOPK // (a + 1)
                n_a = jnp.sum(self32[n:n + width, :], axis=0, keepdims=True)
                n1 = jnp.where(rank1 == float(a), n_a, n1)
                n += width
            e1_ref[h] = jnp.exp(s1 - v1_ref[0:1, :]) * inv_z
            n1_ref[h] = n1
            r2_ref[h] = rk2_ref[...].astype(BF16)
            e2_ref[h] = jnp.exp(s2 - v2_ref[0:1, :]).astype(BF16)
            return carry

        lax.fori_loop(0, PEER_HEADS, head, 0)

    ht = lax.dot_general(u_ref[...], xn, _NT, preferred_element_type=F32)
    ps = []
    for kk in range(PEER_IBLK):
        i = ib * PEER_IBLK + kk
        hk = ht[kk * N_KEYS:(kk + 1) * N_KEYS, :]
        act = 0.5 * hk * (1.0 + lax.erf(hk * 0.7071067811865476))
        g = jnp.zeros((N_KEYS, t), BF16)
        for h in range(PEER_HEADS):
            e1row = jnp.broadcast_to(e1_ref[h, pl.ds(i, 1), :].astype(BF16), (N_KEYS, t))
            n1row = jnp.broadcast_to(n1_ref[h, pl.ds(i, 1), :].astype(BF16), (N_KEYS, t))
            g = g + e1row * jnp.where(r2_ref[h] < n1row, e2_ref[h], jnp.zeros((), BF16))
        ps.append(g * act.astype(BF16))
    p = jnp.concatenate(ps, axis=0)
    yt_ref[...] += jnp.dot(vt_ref[...], p, preferred_element_type=F32)

    @pl.when(ib == nib - 1)
    def _():
        hfin = h1_ref[...] + yt_ref[...].T
        out_ref[...] = _rms(hfin, fw_ref[...])


def _peer(xn2, h1, wqt, k1, k2, u, vt, fw):
    n = xn2.shape[0]
    t = PEER_TILE
    eb = PEER_IBLK * N_KEYS
    nib = (N_KEYS * N_KEYS) // eb
    tok = pl.BlockSpec((t, D_MODEL), lambda i, j: (i, 0))
    big = pltpu.VMEM((PEER_HEADS, N_KEYS, t), F32)
    bigb = pltpu.VMEM((PEER_HEADS, N_KEYS, t), BF16)
    keys = pltpu.VMEM((N_KEYS, t), F32)
    return pl.pallas_call(
        _peer_kernel,
        grid=(n // t, nib),
        in_specs=[tok, tok, _const_spec(wqt.shape), _const_spec(k1.shape), _const_spec(k2.shape),
                  pl.BlockSpec((eb, D_MODEL), lambda i, j: (j, 0)),
                  pl.BlockSpec((D_MODEL, eb), lambda i, j: (0, j)),
                  _const_spec(fw.shape)],
        out_specs=tok,
        out_shape=jax.ShapeDtypeStruct((n, D_MODEL), F32),
        scratch_shapes=[big, big, bigb, bigb, keys, keys,
                        pltpu.VMEM((PEER_TOPK, t), F32), pltpu.VMEM((PEER_TOPK, t), F32),
                        pltpu.VMEM((PEER_CAND_ROWS, t), F32), pltpu.VMEM((PEER_CAND_ROWS, t), F32),
                        pltpu.VMEM((D_MODEL, t), F32)],
        compiler_params=_params("parallel", "arbitrary"),
        name="peer",
    )(xn2, h1, wqt, k1, k2, u, vt, fw)


def kernel(x, mix_norm_w, w_in, pool_w, pool_scale, conv_w, a_log, dt_bias, dn_norm_w, w_pool_up,
           w_dn_up, w_mix_out, ffn_norm_w, peer_w_query, peer_keys_1, peer_keys_2, peer_down, peer_up,
           final_norm_w):
    b, s, d = x.shape
    n = b * s
    assert d == D_MODEL and s % POOL_TILE == 0 and s % DN_CHUNK == 0
    assert n % IN_TILE == 0 and n % POST_TILE == 0 and n % PEER_TILE == 0
    assert w_in.shape[0] == 1, "single-layer block"
    l = 0
    h = x.reshape(n, d)

    c0 = POOL_WIDTH
    c1 = c0 + 3 * DN_WIDTH
    c2 = c1 + DN_WIDTH
    c3 = c2 + 2 * DN_HEADS
    c4 = c3 + D_MODEL
    wi = w_in[l]
    xa, qkv, z, ga, gb, bd = _in_proj(
        h, mix_norm_w[l].reshape(1, d),
        wi[:, :c0].astype(BF16), wi[:, c0:c1].astype(BF16), wi[:, c1:c2].astype(BF16),
        wi[:, c3:c4].astype(BF16), wi[:, c4:].astype(BF16), wi[:, c2:c3])

    ya = _pool(xa.reshape(b, s, POOL_WIDTH), pool_w[l].astype(BF16), pool_scale[l].reshape(1, POOL_WIDTH),
               w_pool_up[l].astype(BF16))

    bd3 = bd.reshape(b, s, 2 * DN_HEADS)
    o = _deltanet(qkv.reshape(b, s, 3 * DN_WIDTH), bd3, bd3.transpose(0, 2, 1), conv_w[l], a_log[l],
                  dt_bias[l])

    h1, xn2 = _post(o.reshape(n, DN_WIDTH), z, ga, gb, ya.reshape(n, d), h,
                    dn_norm_w[l].reshape(1, DN_HEAD_DIM), w_dn_up[l].astype(BF16),
                    w_mix_out[l].astype(BF16), ffn_norm_w[l].reshape(1, d))

    out = _peer(xn2, h1, peer_w_query[l].T.astype(BF16), peer_keys_1[l].astype(BF16),
                peer_keys_2[l].astype(BF16), peer_down[l].astype(BF16), peer_up[l].T.astype(BF16),
                final_norm_w.reshape(1, d))
    return out.reshape(b, s, d)
```

```python
import functools

import jax
import jax.numpy as jnp
from jax import lax
from jax.experimental import pallas as pl
from jax.experimental.pallas import tpu as pltpu

F32 = jnp.float32
BF16 = jnp.bfloat16
EPS = 1e-6

D_MODEL = 1024
POOL_WINDOWS = (2, 4, 8, 16)
POOL_GROUP_DIM = 128
POOL_WIDTH = 512
POOL_HALO = 16
DN_HEADS = 8
DN_HEAD_DIM = 128
DN_WIDTH = DN_HEADS * DN_HEAD_DIM
CONV_WIDTH = 4
CONV_HALO = 8
DN_CHUNK = 128
PEER_HEADS = 8
N_KEYS = 128
PEER_TOPK = 16
PEER_HALF = 128

V7X_LANES = 128
BF16_SUBLANES = 16
VMEM_LIMIT = 56 * 1024 * 1024

IN_TILE = 256
POOL_TILE = 512
POST_TILE = 256
PEER_TILE = 512
PEER_IBLK = 4

PEER_CANDS = tuple((a, b) for a in range(PEER_TOPK) for b in range(PEER_TOPK)
                   if (a + 1) * (b + 1) <= PEER_TOPK)
PEER_CAND_ROWS = 56

_NT = (((1,), (1,)), ((), ()))
_TN = (((0,), (0,)), ((), ()))


def _params(*sem):
    return pltpu.CompilerParams(dimension_semantics=sem, vmem_limit_bytes=VMEM_LIMIT)


def _const_spec(shape):
    return pl.BlockSpec(shape, lambda *_: (0,) * len(shape))


def _sigmoid(x):
    return 1.0 / (1.0 + jnp.exp(-x))


def _in_proj_kernel(x_ref, nw_ref, wxa_ref, wqkv_ref, wz_ref, wga_ref, wgb_ref, wbd_ref,
                    xa_ref, qkv_ref, z_ref, ga_ref, gb_ref, bd_ref):
    x = x_ref[...]
    xn = x * lax.rsqrt(jnp.mean(x * x, axis=-1, keepdims=True) + EPS) * nw_ref[...]
    xb = xn.astype(BF16)
    xa_ref[...] = jnp.dot(xb, wxa_ref[...], preferred_element_type=F32)
    qkv_ref[...] = jnp.dot(xb, wqkv_ref[...], preferred_element_type=F32)
    z_ref[...] = jnp.dot(xb, wz_ref[...], preferred_element_type=F32)
    ga_ref[...] = jnp.dot(xb, wga_ref[...], preferred_element_type=F32)
    gb_ref[...] = jnp.dot(xb, wgb_ref[...], preferred_element_type=F32)
    bd_ref[...] = jnp.dot(xn, wbd_ref[...], preferred_element_type=F32,
                          precision=lax.Precision.HIGHEST)


def _in_proj(x2, nw, wxa, wqkv, wz, wga, wgb, wbd):
    n = x2.shape[0]
    t = IN_TILE
    row = lambda w: pl.BlockSpec((t, w), lambda i: (i, 0))
    widths = (POOL_WIDTH, 3 * DN_WIDTH, DN_WIDTH, D_MODEL, D_MODEL, 2 * DN_HEADS)
    return pl.pallas_call(
        _in_proj_kernel,
        grid=(n // t,),
        in_specs=[row(D_MODEL), _const_spec(nw.shape), _const_spec(wxa.shape), _const_spec(wqkv.shape),
                  _const_spec(wz.shape), _const_spec(wga.shape), _const_spec(wgb.shape),
                  _const_spec(wbd.shape)],
        out_specs=[row(w) for w in widths],
        out_shape=[jax.ShapeDtypeStruct((n, w), F32) for w in widths],
        compiler_params=_params("parallel"),
        name="in_proj",
    )(x2, nw, wxa, wqkv, wz, wga, wgb, wbd)


def _pool_kernel(xa_ref, pw_ref, ps_ref, wup_ref, ya_ref, buf_ref):
    s = pl.program_id(1)
    t = xa_ref.shape[1]

    @pl.when(s == 0)
    def _():
        buf_ref[0:POOL_HALO, :] = jnp.zeros((POOL_HALO, POOL_WIDTH), F32)

    @pl.when(s > 0)
    def _():
        buf_ref[0:POOL_HALO, :] = buf_ref[t:t + POOL_HALO, :]

    buf_ref[POOL_HALO:, :] = xa_ref[0]
    pos = s * t + lax.broadcasted_iota(jnp.int32, (t, 1), 0)
    ys = []
    for g, win in enumerate(POOL_WINDOWS):
        sl = slice(g * POOL_GROUP_DIM, (g + 1) * POOL_GROUP_DIM)
        xg = buf_ref[POOL_HALO:POOL_HALO + t, sl]
        acc = xg
        for k in range(1, win):
            acc = acc + buf_ref[POOL_HALO - k:POOL_HALO - k + t, sl]
        cnt = jnp.minimum(pos + 1, win).astype(F32)
        pooled = acc / cnt - xg
        y = jnp.dot(pooled.astype(BF16), pw_ref[g], preferred_element_type=F32)
        ys.append(y * ps_ref[:, sl])
    y = jnp.concatenate(ys, axis=-1)
    ya_ref[0] = jnp.dot(y.astype(BF16), wup_ref[...], preferred_element_type=F32)


def _pool(xa, pw, ps, wup):
    b, s, _ = xa.shape
    t = POOL_TILE
    return pl.pallas_call(
        _pool_kernel,
        grid=(b, s // t),
        in_specs=[pl.BlockSpec((1, t, POOL_WIDTH), lambda i, j: (i, j, 0)),
                  _const_spec(pw.shape), _const_spec(ps.shape), _const_spec(wup.shape)],
        out_specs=pl.BlockSpec((1, t, D_MODEL), lambda i, j: (i, j, 0)),
        out_shape=jax.ShapeDtypeStruct((b, s, D_MODEL), F32),
        scratch_shapes=[pltpu.VMEM((t + POOL_HALO, POOL_WIDTH), F32)],
        compiler_params=_params("parallel", "arbitrary"),
        name="pool",
    )(xa, pw, ps, wup)


def _softplus(x):
    return jnp.maximum(x, 0.0) + jnp.log1p(jnp.exp(-jnp.abs(x)))


def _dn_kernel(qkv_ref, bd_ref, bdt_ref, cw_ref, alog_ref, dtb_ref, alogt_ref, dtbt_ref,
               o_ref, cbuf_ref, st_ref):
    s = pl.program_id(1)
    c = DN_CHUNK
    hd = DN_HEAD_DIM

    @pl.when(s == 0)
    def _():
        cbuf_ref[0:CONV_HALO, :] = jnp.zeros((CONV_HALO, 3 * DN_WIDTH), F32)
        st_ref[...] = jnp.zeros(st_ref.shape, F32)

    @pl.when(s > 0)
    def _():
        cbuf_ref[0:CONV_HALO, :] = cbuf_ref[c:c + CONV_HALO, :]

    cbuf_ref[CONV_HALO:, :] = qkv_ref[0]

    bd = bd_ref[0]
    bdt = bdt_ref[0]
    beta_all = _sigmoid(bd[:, 0:DN_HEADS])
    g_all = -jnp.exp(alog_ref[...]) * _softplus(bd[:, DN_HEADS:] + dtb_ref[...])
    g_allt = -jnp.exp(alogt_ref[...]) * _softplus(bdt[DN_HEADS:, :] + dtbt_ref[...])

    row = lax.broadcasted_iota(jnp.int32, (c, c), 0)
    col = lax.broadcasted_iota(jnp.int32, (c, c), 1)
    tril = row >= col
    strict = row > col
    gc = jnp.dot(tril.astype(F32), g_all, preferred_element_type=F32, precision=lax.Precision.HIGHEST)
    gct = jnp.dot(g_allt, (row <= col).astype(F32), preferred_element_type=F32,
                  precision=lax.Precision.HIGHEST)
    eye = (row == col).astype(F32)

    def conv_silu(lane0):
        sl = slice(lane0, lane0 + hd)
        y = cw_ref[0:1, sl] * cbuf_ref[CONV_HALO - 3:CONV_HALO - 3 + c, sl]
        for j in range(1, CONV_WIDTH):
            y = y + cw_ref[j:j + 1, sl] * cbuf_ref[CONV_HALO - 3 + j:CONV_HALO - 3 + j + c, sl]
        return y * _sigmoid(y)

    def l2n(v):
        return v * lax.rsqrt(jnp.sum(v * v, axis=-1, keepdims=True) + EPS)

    hs = range(DN_HEADS)
    q = [l2n(conv_silu(h * hd)) * (hd ** -0.5) for h in hs]
    k = [l2n(conv_silu(DN_WIDTH + h * hd)) for h in hs]
    v = [conv_silu(2 * DN_WIDTH + h * hd) for h in hs]
    gcol = [gc[:, h:h + 1] for h in hs]
    grow = [gct[h:h + 1, :] for h in hs]
    decay = [jnp.where(tril, jnp.exp(jnp.where(tril, gcol[h] - grow[h], 0.0)), 0.0) for h in hs]
    kb = [k[h] * beta_all[:, h:h + 1] for h in hs]
    vb = [v[h] * beta_all[:, h:h + 1] for h in hs]
    kf = [k[h].astype(BF16) for h in hs]
    lower = [jnp.where(strict, lax.dot_general(kb[h].astype(BF16), kf[h], _NT, preferred_element_type=F32)
                       * decay[h], 0.0) for h in hs]
    attn = [(lax.dot_general(q[h].astype(BF16), kf[h], _NT, preferred_element_type=F32) * decay[h]).astype(BF16)
            for h in hs]
    pair = ((row ^ col) == 1) & strict
    tinv = [eye - jnp.where(pair, lower[h], 0.0) for h in hs]
    blk = 2
    while blk < c:
        m = ((row ^ col) < 2 * blk) & ((row & blk) != 0) & ((col & blk) == 0)
        tb = [tinv[h].astype(BF16) for h in hs]
        left = [jnp.dot(tb[h], jnp.where(m, lower[h], 0.0).astype(BF16), preferred_element_type=F32).astype(BF16)
                for h in hs]
        tinv = [tinv[h] - jnp.dot(left[h], tb[h], preferred_element_type=F32) for h in hs]
        blk *= 2
    eg = [jnp.exp(gcol[h]) for h in hs]
    uw = [jnp.dot(tinv[h].astype(BF16),
                  jnp.concatenate([vb[h], kb[h] * eg[h]], axis=-1).astype(BF16), preferred_element_type=F32)
          for h in hs]
    st = [st_ref[h] for h in hs]
    ws = [jnp.dot(jnp.concatenate([uw[h][:, hd:], q[h] * eg[h]], axis=0).astype(BF16), st[h].astype(BF16),
                  preferred_element_type=F32) for h in hs]
    vnb = [(uw[h][:, :hd] - ws[h][:c]).astype(BF16) for h in hs]
    o = [ws[h][c:] + jnp.dot(attn[h], vnb[h], preferred_element_type=F32) for h in hs]
    for h in hs:
        g_last = grow[h][:, c - 1:c]
        k_dec = (k[h] * jnp.exp(g_last - gcol[h])).astype(BF16)
        st_ref[h] = st[h] * jnp.exp(g_last) + lax.dot_general(k_dec, vnb[h], _TN, preferred_element_type=F32)
    o_ref[0] = jnp.concatenate(o, axis=-1)


def _deltanet(qkv, bd, bdt, cw, alog, dtb):
    b, s, _ = qkv.shape
    c = DN_CHUNK
    alog2, dtb2 = alog.reshape(1, DN_HEADS), dtb.reshape(1, DN_HEADS)
    alogt, dtbt = alog.reshape(DN_HEADS, 1), dtb.reshape(DN_HEADS, 1)
    return pl.pallas_call(
        _dn_kernel,
        grid=(b, s // c),
        in_specs=[pl.BlockSpec((1, c, 3 * DN_WIDTH), lambda i, j: (i, j, 0)),
                  pl.BlockSpec((1, c, 2 * DN_HEADS), lambda i, j: (i, j, 0)),
                  pl.BlockSpec((1, 2 * DN_HEADS, c), lambda i, j: (i, 0, j)),
                  _const_spec(cw.shape), _const_spec(alog2.shape), _const_spec(dtb2.shape),
                  _const_spec(alogt.shape), _const_spec(dtbt.shape)],
        out_specs=pl.BlockSpec((1, c, DN_WIDTH), lambda i, j: (i, j, 0)),
        out_shape=jax.ShapeDtypeStruct((b, s, DN_WIDTH), F32),
        scratch_shapes=[pltpu.VMEM((c + CONV_HALO, 3 * DN_WIDTH), F32),
                        pltpu.VMEM((DN_HEADS, DN_HEAD_DIM, DN_HEAD_DIM), F32)],
        compiler_params=_params("parallel", "arbitrary"),
        name="deltanet",
    )(qkv, bd, bdt, cw, alog2, dtb2, alogt, dtbt)


def _rms(x, w):
    return x * lax.rsqrt(jnp.mean(x * x, axis=-1, keepdims=True) + EPS) * w


def _post_kernel(o_ref, z_ref, ga_ref, gb_ref, ya_ref, x_ref, dnw_ref, wdn_ref, wmix_ref, fnw_ref,
                 h1_ref, xn2t_ref):
    o = o_ref[...]
    z = z_ref[...]
    parts = []
    for h in range(DN_HEADS):
        sl = slice(h * DN_HEAD_DIM, (h + 1) * DN_HEAD_DIM)
        parts.append(_rms(o[:, sl], dnw_ref[...]))
    on = jnp.concatenate(parts, axis=-1) * (z * _sigmoid(z))
    yb = jnp.dot(on.astype(BF16), wdn_ref[...], preferred_element_type=F32)
    merged = _sigmoid(ga_ref[...]) * ya_ref[...] + _sigmoid(gb_ref[...]) * yb
    h1 = x_ref[...] + jnp.dot(merged.astype(BF16), wmix_ref[...], preferred_element_type=F32)
    h1_ref[...] = h1
    xn2t_ref[...] = _rms(h1, fnw_ref[...]).T.astype(BF16)


def _post(o, z, ga, gb, ya, x2, dnw, wdn, wmix, fnw):
    n = x2.shape[0]
    t = POST_TILE
    row = pl.BlockSpec((t, D_MODEL), lambda i: (i, 0))
    return pl.pallas_call(
        _post_kernel,
        grid=(n // t,),
        in_specs=[row, row, row, row, row, row, _const_spec(dnw.shape), _const_spec(wdn.shape),
                  _const_spec(wmix.shape), _const_spec(fnw.shape)],
        out_specs=[row, pl.BlockSpec((D_MODEL, t), lambda i: (0, i))],
        out_shape=[jax.ShapeDtypeStruct((n, D_MODEL), F32), jax.ShapeDtypeStruct((D_MODEL, n), BF16)],
        compiler_params=_params("parallel"),
        name="post",
    )(o, z, ga, gb, ya, x2, dnw, wdn, wmix, fnw)


_INT_MIN = -2 ** 31


def _order_key(x):
    b = lax.bitcast_convert_type(x, jnp.int32)
    return b ^ ((b >> 31) & 0x7FFFFFFF)


def _key_value(key):
    return lax.bitcast_convert_type(key ^ ((key >> 31) & 0x7FFFFFFF), F32)


def _extract_fast(x, rounds, vals_ref=None):
    ranks, bad = [], []
    for l0 in range(0, x.shape[1], V7X_LANES):
        lanes = slice(l0, l0 + V7X_LANES)
        key = _order_key(x[:, lanes])
        for r in range(rounds):
            m = jnp.max(key, axis=0, keepdims=True)
            if vals_ref is not None:
                vals_ref[r:r + 1, lanes] = _key_value(m)
            key = jnp.where(key == m, _INT_MIN + r, key)
        hit = key < _INT_MIN + rounds
        ranks.append(jnp.where(hit, key - _INT_MIN, rounds).astype(F32))
        bad.append(jnp.abs(jnp.sum(hit.astype(F32), axis=0, keepdims=True) - float(rounds)))
    ok = jnp.max(jnp.concatenate(bad, axis=1)) == 0.0
    return jnp.concatenate(ranks, axis=1), ok


def _extract_exact(work, rounds, vals_ref=None):
    rows = work.shape[0]
    iota = lax.broadcasted_iota(jnp.int32, work.shape, 0).astype(F32)
    rank = jnp.full(work.shape, float(rounds), F32)
    for r in range(rounds):
        m = jnp.max(work, axis=0, keepdims=True)
        idx = jnp.min(jnp.where(work == m, iota, float(rows)), axis=0, keepdims=True)
        hit = iota == idx
        rank = jnp.where(hit, float(r), rank)
        work = jnp.where(hit, -jnp.inf, work)
        if vals_ref is not None:
            vals_ref[r:r + 1, :] = m
    return rank


def _bcast_rows_bf16(row, rows):
    packed = jnp.broadcast_to(row, (BF16_SUBLANES, row.shape[1])).astype(BF16)
    return jnp.concatenate([packed] * (rows // BF16_SUBLANES), axis=0)


def _peer_kernel(xnt_ref, h1_ref, wq_ref, k1_ref, k2_ref, u_first_ref, u_odd_ref, u_even_ref,
                 vt_odd_ref, vt_even_ref, vt_last_ref, fw_ref, out_ref,
                 e1_ref, n1_ref, r2_ref, e2_ref, rk1_ref, rk2_ref, v1_ref, v2_ref, cand_ref, sel_ref, yt_ref,
                 h_even_ref, h_odd_ref, p_even_ref, p_odd_ref):
    ib = pl.program_id(1)
    nib = pl.num_programs(1)
    t = xnt_ref.shape[1]

    @pl.when(ib == 0)
    def _():
        yt_ref[...] = jnp.zeros(yt_ref.shape, F32)
        p_odd_ref[...] = jnp.zeros(p_odd_ref.shape, BF16)
        xnt = xnt_ref[...]
        h_even_ref[...] = jnp.dot(u_first_ref[...], xnt, preferred_element_type=F32)

        def head(h, carry):
            r0 = pl.multiple_of(h * 2 * PEER_HALF, 2 * PEER_HALF)
            q1 = jnp.dot(wq_ref[pl.ds(r0, PEER_HALF), :], xnt, preferred_element_type=F32)
            q2 = jnp.dot(wq_ref[pl.ds(r0 + PEER_HALF, PEER_HALF), :], xnt, preferred_element_type=F32)
            s1 = jnp.dot(k1_ref[h], q1.astype(BF16), preferred_element_type=F32)
            s2 = jnp.dot(k2_ref[h], q2.astype(BF16), preferred_element_type=F32)
            rank1, ok1 = _extract_fast(s1, PEER_TOPK, v1_ref)
            rank2, ok2 = _extract_fast(s2, PEER_TOPK, v2_ref)
            rk1_ref[...] = rank1
            rk2_ref[...] = rank2

            @pl.when(jnp.logical_not(ok1 & ok2))
            def _():
                rk1_ref[...] = _extract_exact(s1, PEER_TOPK, v1_ref)
                rk2_ref[...] = _extract_exact(s2, PEER_TOPK, v2_ref)

            cand_ref[...] = jnp.full(cand_ref.shape, -jnp.inf, F32)
            for n, (a, b) in enumerate(PEER_CANDS):
                cand_ref[n:n + 1, :] = v1_ref[a:a + 1, :] + v2_ref[b:b + 1, :]
            cand = cand_ref[...]
            crank, ok3 = _extract_fast(cand, PEER_TOPK)
            sel_ref[...] = crank

            @pl.when(jnp.logical_not(ok3))
            def _():
                sel_ref[...] = _extract_exact(cand, PEER_TOPK)

            sel = sel_ref[...] < float(PEER_TOPK)
            cexp = jnp.where(sel, jnp.exp(jnp.where(sel, cand - cand[0:1, :], 0.0)), 0.0)
            inv_z = 1.0 / jnp.sum(cexp, axis=0, keepdims=True)
            self32 = sel.astype(F32)
            rank1 = rk1_ref[...]
            n1 = jnp.zeros(rank1.shape, F32)
            n = 0
            for a in range(PEER_TOPK):
                width = PEER_TOPK // (a + 1)
                n_a = jnp.sum(self32[n:n + width, :], axis=0, keepdims=True)
                n1 = jnp.where(rank1 == float(a), n_a, n1)
                n += width
            e1_ref[h] = jnp.exp(s1 - v1_ref[0:1, :]) * inv_z
            n1_ref[h] = n1
            r2_ref[h] = rk2_ref[...].astype(BF16)
            e2_ref[h] = jnp.exp(s2 - v2_ref[0:1, :]).astype(BF16)
            return carry

        lax.fori_loop(0, PEER_HEADS, head, 0)

    th = t // 2

    def stage(blk, hf, u_next_ref, h_next_ref, h_cur_ref, p_cur_ref, vt_prev_ref, p_prev_ref):
        lanes = slice(hf * th, (hf + 1) * th)
        h_next_ref[:, lanes] = jnp.dot(u_next_ref[...], xnt_ref[:, lanes], preferred_element_type=F32)
        for kk in range(PEER_IBLK):
            i = blk * PEER_IBLK + kk
            rows = slice(kk * N_KEYS, (kk + 1) * N_KEYS)
            hk = h_cur_ref[rows, lanes]
            act = 0.5 * hk * (1.0 + lax.erf(hk * 0.7071067811865476))
            g = jnp.zeros((N_KEYS, th), BF16)
            for h in range(PEER_HEADS):
                e1row = _bcast_rows_bf16(e1_ref[h, pl.ds(i, 1), lanes], N_KEYS)
                n1row = _bcast_rows_bf16(n1_ref[h, pl.ds(i, 1), lanes], N_KEYS)
                g = g + e1row * jnp.where(r2_ref[h, :, lanes] < n1row, e2_ref[h, :, lanes], jnp.zeros((), BF16))
            p_cur_ref[rows, lanes] = g * act.astype(BF16)
        yt_ref[:, lanes] += jnp.dot(vt_prev_ref[...], p_prev_ref[:, lanes], preferred_element_type=F32)

    for hf in range(2):
        stage(2 * ib, hf, u_odd_ref, h_odd_ref, h_even_ref, p_even_ref, vt_odd_ref, p_odd_ref)
    for hf in range(2):
        stage(2 * ib + 1, hf, u_even_ref, h_even_ref, h_odd_ref, p_odd_ref, vt_even_ref, p_even_ref)

    @pl.when(ib == nib - 1)
    def _():
        yt = yt_ref[...] + jnp.dot(vt_last_ref[...], p_odd_ref[...], preferred_element_type=F32)
        hfin = h1_ref[...] + yt.T
        out_ref[...] = _rms(hfin, fw_ref[...])


def _peer(xn2t, h1, wqt, k1, k2, u, vt, fw):
    n = xn2t.shape[1]
    t = PEER_TILE
    eb = PEER_IBLK * N_KEYS
    nblk = (N_KEYS * N_KEYS) // eb
    assert nblk % 2 == 0
    tok = pl.BlockSpec((t, D_MODEL), lambda i, j: (i, 0))
    big = pltpu.VMEM((PEER_HEADS, N_KEYS, t), F32)
    bigb = pltpu.VMEM((PEER_HEADS, N_KEYS, t), BF16)
    keys = pltpu.VMEM((N_KEYS, t), F32)
    u_spec = lambda f: pl.BlockSpec((eb, D_MODEL), lambda i, j: (f(j), 0))
    vt_spec = lambda f: pl.BlockSpec((D_MODEL, eb), lambda i, j: (0, f(j)))
    return pl.pallas_call(
        _peer_kernel,
        grid=(n // t, nblk // 2),
        in_specs=[pl.BlockSpec((D_MODEL, t), lambda i, j: (0, i)), tok,
                  _const_spec(wqt.shape), _const_spec(k1.shape), _const_spec(k2.shape),
                  u_spec(lambda j: 0), u_spec(lambda j: 2 * j + 1),
                  u_spec(lambda j: jnp.minimum(2 * j + 2, nblk - 1)),
                  vt_spec(lambda j: jnp.maximum(2 * j - 1, 0)), vt_spec(lambda j: 2 * j),
                  vt_spec(lambda j: nblk - 1),
                  _const_spec(fw.shape)],
        out_specs=tok,
        out_shape=jax.ShapeDtypeStruct((n, D_MODEL), F32),
        scratch_shapes=[big, big, bigb, bigb, keys, keys,
                        pltpu.VMEM((PEER_TOPK, t), F32), pltpu.VMEM((PEER_TOPK, t), F32),
                        pltpu.VMEM((PEER_CAND_ROWS, t), F32), pltpu.VMEM((PEER_CAND_ROWS, t), F32),
                        pltpu.VMEM((D_MODEL, t), F32),
                        pltpu.VMEM((eb, t), F32), pltpu.VMEM((eb, t), F32),
                        pltpu.VMEM((eb, t), BF16), pltpu.VMEM((eb, t), BF16)],
        compiler_params=_params("parallel", "arbitrary"),
        name="peer",
    )(xn2t, h1, wqt, k1, k2, u, u, u, vt, vt, vt, fw)


def kernel(x, mix_norm_w, w_in, pool_w, pool_scale, conv_w, a_log, dt_bias, dn_norm_w, w_pool_up,
           w_dn_up, w_mix_out, ffn_norm_w, peer_w_query, peer_keys_1, peer_keys_2, peer_down, peer_up,
           final_norm_w):
    b, s, d = x.shape
    n = b * s
    assert d == D_MODEL and s % POOL_TILE == 0 and s % DN_CHUNK == 0
    assert n % IN_TILE == 0 and n % POST_TILE == 0 and n % PEER_TILE == 0
    assert w_in.shape[0] == 1, "single-layer block"
    l = 0
    h = x.reshape(n, d)

    c0 = POOL_WIDTH
    c1 = c0 + 3 * DN_WIDTH
    c2 = c1 + DN_WIDTH
    c3 = c2 + 2 * DN_HEADS
    c4 = c3 + D_MODEL
    wi = w_in[l]
    xa, qkv, z, ga, gb, bd = _in_proj(
        h, mix_norm_w[l].reshape(1, d),
        wi[:, :c0].astype(BF16), wi[:, c0:c1].astype(BF16), wi[:, c1:c2].astype(BF16),
        wi[:, c3:c4].astype(BF16), wi[:, c4:].astype(BF16), wi[:, c2:c3])

    ya = _pool(xa.reshape(b, s, POOL_WIDTH), pool_w[l].astype(BF16), pool_scale[l].reshape(1, POOL_WIDTH),
               w_pool_up[l].astype(BF16))

    bd3 = bd.reshape(b, s, 2 * DN_HEADS)
    o = _deltanet(qkv.reshape(b, s, 3 * DN_WIDTH), bd3, bd3.transpose(0, 2, 1), conv_w[l], a_log[l],
                  dt_bias[l])

    h1, xn2t = _post(o.reshape(n, DN_WIDTH), z, ga, gb, ya.reshape(n, d), h,
                    dn_norm_w[l].reshape(1, DN_HEAD_DIM), w_dn_up[l].astype(BF16),
                    w_mix_out[l].astype(BF16), ffn_norm_w[l].reshape(1, d))

    out = _peer(xn2t, h1, peer_w_query[l].T.astype(BF16), peer_keys_1[l].astype(BF16),
                peer_keys_2[l].astype(BF16), peer_down[l].astype(BF16), peer_up[l].T.astype(BF16),
                final_norm_w.reshape(1, d))
    return out.reshape(b, s, d)
```

```python
import functools

import jax
import jax.numpy as jnp
from jax import lax
from jax.experimental import pallas as pl
from jax.experimental.pallas import tpu as pltpu

F32 = jnp.float32
BF16 = jnp.bfloat16
EPS = 1e-6

D_MODEL = 1024
POOL_WINDOWS = (2, 4, 8, 16)
POOL_GROUP_DIM = 128
POOL_WIDTH = 512
POOL_HALO = 16
DN_HEADS = 8
DN_HEAD_DIM = 128
DN_WIDTH = DN_HEADS * DN_HEAD_DIM
CONV_WIDTH = 4
CONV_HALO = 8
DN_CHUNK = 128
PEER_HEADS = 8
N_KEYS = 128
PEER_TOPK = 16
PEER_HALF = 128

V7X_LANES = 128
BF16_SUBLANES = 16
VMEM_LIMIT = 56 * 1024 * 1024

IN_TILE = 512
POOL_TILE = 512
POST_TILE = 256
PEER_TILE = 512
PEER_IBLK = 4

PEER_CANDS = tuple((a, b) for a in range(PEER_TOPK) for b in range(PEER_TOPK)
                   if (a + 1) * (b + 1) <= PEER_TOPK)
PEER_CAND_ROWS = 56

_NT = (((1,), (1,)), ((), ()))
_TN = (((0,), (0,)), ((), ()))


def _params(*sem):
    return pltpu.CompilerParams(dimension_semantics=sem, vmem_limit_bytes=VMEM_LIMIT)


def _const_spec(shape):
    return pl.BlockSpec(shape, lambda *_: (0,) * len(shape))


def _resident_spec(shape):
    return pl.BlockSpec(shape, lambda *_: (0,) * len(shape), pipeline_mode=pl.Buffered(1))


def _sigmoid(x):
    return 1.0 / (1.0 + jnp.exp(-x))


def _in_proj_kernel(x_ref, nw_ref, wxa_ref, wqkv_ref, wz_ref, wga_ref, wgb_ref, wbd_ref,
                    xa_ref, qkv_ref, z_ref, ga_ref, gb_ref, bd_ref):
    x = x_ref[...]
    xn = x * lax.rsqrt(jnp.mean(x * x, axis=-1, keepdims=True) + EPS) * nw_ref[...]
    xb = xn.astype(BF16)
    xa_ref[...] = jnp.dot(xb, wxa_ref[...], preferred_element_type=F32)
    qkv_ref[...] = jnp.dot(xb, wqkv_ref[...], preferred_element_type=F32)
    z_ref[...] = jnp.dot(xb, wz_ref[...], preferred_element_type=F32).astype(BF16)
    ga_ref[...] = jnp.dot(xb, wga_ref[...], preferred_element_type=F32).astype(BF16)
    gb_ref[...] = jnp.dot(xb, wgb_ref[...], preferred_element_type=F32).astype(BF16)
    bd_ref[...] = jnp.dot(xn, wbd_ref[...], preferred_element_type=F32,
                          precision=lax.Precision.HIGHEST)


def _in_proj(x2, nw, wxa, wqkv, wz, wga, wgb, wbd):
    n = x2.shape[0]
    t = IN_TILE
    row = lambda w: pl.BlockSpec((t, w), lambda i: (i, 0))
    widths = (POOL_WIDTH, 3 * DN_WIDTH, DN_WIDTH, D_MODEL, D_MODEL, 2 * DN_HEADS)
    dtypes = (F32, F32, BF16, BF16, BF16, F32)
    return pl.pallas_call(
        _in_proj_kernel,
        grid=(n // t,),
        in_specs=[row(D_MODEL), _const_spec(nw.shape), _resident_spec(wxa.shape), _resident_spec(wqkv.shape),
                  _resident_spec(wz.shape), _resident_spec(wga.shape), _resident_spec(wgb.shape),
                  _const_spec(wbd.shape)],
        out_specs=[row(w) for w in widths],
        out_shape=[jax.ShapeDtypeStruct((n, w), dt) for w, dt in zip(widths, dtypes)],
        compiler_params=_params("parallel"),
        name="in_proj",
    )(x2, nw, wxa, wqkv, wz, wga, wgb, wbd)


def _pool_kernel(xa_ref, pw_ref, ps_ref, wup_ref, ya_ref, buf_ref):
    s = pl.program_id(1)
    t = xa_ref.shape[1]

    @pl.when(s == 0)
    def _():
        buf_ref[0:POOL_HALO, :] = jnp.zeros((POOL_HALO, POOL_WIDTH), F32)

    @pl.when(s > 0)
    def _():
        buf_ref[0:POOL_HALO, :] = buf_ref[t:t + POOL_HALO, :]

    buf_ref[POOL_HALO:, :] = xa_ref[0]
    pos = s * t + lax.broadcasted_iota(jnp.int32, (t, 1), 0)
    ys = []
    for g, win in enumerate(POOL_WINDOWS):
        sl = slice(g * POOL_GROUP_DIM, (g + 1) * POOL_GROUP_DIM)
        xg = buf_ref[POOL_HALO:POOL_HALO + t, sl]
        acc = xg
        for k in range(1, win):
            acc = acc + buf_ref[POOL_HALO - k:POOL_HALO - k + t, sl]
        cnt = jnp.minimum(pos + 1, win).astype(F32)
        pooled = acc / cnt - xg
        y = jnp.dot(pooled.astype(BF16), pw_ref[g], preferred_element_type=F32)
        ys.append(y * ps_ref[:, sl])
    y = jnp.concatenate(ys, axis=-1)
    ya_ref[0] = jnp.dot(y.astype(BF16), wup_ref[...], preferred_element_type=F32).astype(BF16)


def _pool(xa, pw, ps, wup):
    b, s, _ = xa.shape
    t = POOL_TILE
    return pl.pallas_call(
        _pool_kernel,
        grid=(b, s // t),
        in_specs=[pl.BlockSpec((1, t, POOL_WIDTH), lambda i, j: (i, j, 0)),
                  _const_spec(pw.shape), _const_spec(ps.shape), _const_spec(wup.shape)],
        out_specs=pl.BlockSpec((1, t, D_MODEL), lambda i, j: (i, j, 0)),
        out_shape=jax.ShapeDtypeStruct((b, s, D_MODEL), BF16),
        scratch_shapes=[pltpu.VMEM((t + POOL_HALO, POOL_WIDTH), F32)],
        compiler_params=_params("parallel", "arbitrary"),
        name="pool",
    )(xa, pw, ps, wup)


def _softplus(x):
    return jnp.maximum(x, 0.0) + jnp.log1p(jnp.exp(-jnp.abs(x)))


def _dn_kernel(qkv_ref, bd_ref, bdt_ref, cw_ref, alog_ref, dtb_ref, alogt_ref, dtbt_ref,
               o_ref, cbuf_ref, st_ref):
    s = pl.program_id(1)
    c = DN_CHUNK
    hd = DN_HEAD_DIM

    @pl.when(s == 0)
    def _():
        cbuf_ref[0:CONV_HALO, :] = jnp.zeros((CONV_HALO, 3 * DN_WIDTH), F32)
        st_ref[...] = jnp.zeros(st_ref.shape, F32)

    @pl.when(s > 0)
    def _():
        cbuf_ref[0:CONV_HALO, :] = cbuf_ref[c:c + CONV_HALO, :]

    cbuf_ref[CONV_HALO:, :] = qkv_ref[0]

    bd = bd_ref[0]
    bdt = bdt_ref[0]
    beta_all = _sigmoid(bd[:, 0:DN_HEADS])
    g_all = -jnp.exp(alog_ref[...]) * _softplus(bd[:, DN_HEADS:] + dtb_ref[...])
    g_allt = -jnp.exp(alogt_ref[...]) * _softplus(bdt[DN_HEADS:, :] + dtbt_ref[...])

    row = lax.broadcasted_iota(jnp.int32, (c, c), 0)
    col = lax.broadcasted_iota(jnp.int32, (c, c), 1)
    tril = row >= col
    strict = row > col
    gc = jnp.dot(tril.astype(F32), g_all, preferred_element_type=F32, precision=lax.Precision.HIGHEST)
    gct = jnp.dot(g_allt, (row <= col).astype(F32), preferred_element_type=F32,
                  precision=lax.Precision.HIGHEST)
    eye = (row == col).astype(F32)

    def conv_silu(lane0):
        sl = slice(lane0, lane0 + hd)
        y = cw_ref[0:1, sl] * cbuf_ref[CONV_HALO - 3:CONV_HALO - 3 + c, sl]
        for j in range(1, CONV_WIDTH):
            y = y + cw_ref[j:j + 1, sl] * cbuf_ref[CONV_HALO - 3 + j:CONV_HALO - 3 + j + c, sl]
        return y * _sigmoid(y)

    def l2n(v):
        return v * lax.rsqrt(jnp.sum(v * v, axis=-1, keepdims=True) + EPS)

    hs = range(DN_HEADS)
    q = [l2n(conv_silu(h * hd)) * (hd ** -0.5) for h in hs]
    k = [l2n(conv_silu(DN_WIDTH + h * hd)) for h in hs]
    v = [conv_silu(2 * DN_WIDTH + h * hd) for h in hs]
    gcol = [gc[:, h:h + 1] for h in hs]
    grow = [gct[h:h + 1, :] for h in hs]
    decay = [jnp.where(tril, jnp.exp(jnp.where(tril, gcol[h] - grow[h], 0.0)), 0.0) for h in hs]
    kb = [k[h] * beta_all[:, h:h + 1] for h in hs]
    vb = [v[h] * beta_all[:, h:h + 1] for h in hs]
    kf = [k[h].astype(BF16) for h in hs]
    lower = [jnp.where(strict, lax.dot_general(kb[h].astype(BF16), kf[h], _NT, preferred_element_type=F32)
                       * decay[h], 0.0) for h in hs]
    attn = [(lax.dot_general(q[h].astype(BF16), kf[h], _NT, preferred_element_type=F32) * decay[h]).astype(BF16)
            for h in hs]
    pair = ((row ^ col) == 1) & strict
    tinv = [eye - jnp.where(pair, lower[h], 0.0) for h in hs]
    blk = 2
    while blk < c:
        m = ((row ^ col) < 2 * blk) & ((row & blk) != 0) & ((col & blk) == 0)
        tb = [tinv[h].astype(BF16) for h in hs]
        left = [jnp.dot(tb[h], jnp.where(m, lower[h], 0.0).astype(BF16), preferred_element_type=F32).astype(BF16)
                for h in hs]
        tinv = [tinv[h] - jnp.dot(left[h], tb[h], preferred_element_type=F32) for h in hs]
        blk *= 2
    eg = [jnp.exp(gcol[h]) for h in hs]
    uw = [jnp.dot(tinv[h].astype(BF16),
                  jnp.concatenate([vb[h], kb[h] * eg[h]], axis=-1).astype(BF16), preferred_element_type=F32)
          for h in hs]
    st = [st_ref[h] for h in hs]
    ws = [jnp.dot(jnp.concatenate([uw[h][:, hd:], q[h] * eg[h]], axis=0).astype(BF16), st[h].astype(BF16),
                  preferred_element_type=F32) for h in hs]
    vnb = [(uw[h][:, :hd] - ws[h][:c]).astype(BF16) for h in hs]
    o = [ws[h][c:] + jnp.dot(attn[h], vnb[h], preferred_element_type=F32) for h in hs]
    for h in hs:
        g_last = grow[h][:, c - 1:c]
        k_dec = (k[h] * jnp.exp(g_last - gcol[h])).astype(BF16)
        st_ref[h] = st[h] * jnp.exp(g_last) + lax.dot_general(k_dec, vnb[h], _TN, preferred_element_type=F32)
    o_ref[0] = jnp.concatenate(o, axis=-1).astype(BF16)


def _deltanet(qkv, bd, bdt, cw, alog, dtb):
    b, s, _ = qkv.shape
    c = DN_CHUNK
    alog2, dtb2 = alog.reshape(1, DN_HEADS), dtb.reshape(1, DN_HEADS)
    alogt, dtbt = alog.reshape(DN_HEADS, 1), dtb.reshape(DN_HEADS, 1)
    return pl.pallas_call(
        _dn_kernel,
        grid=(b, s // c),
        in_specs=[pl.BlockSpec((1, c, 3 * DN_WIDTH), lambda i, j: (i, j, 0)),
                  pl.BlockSpec((1, c, 2 * DN_HEADS), lambda i, j: (i, j, 0)),
                  pl.BlockSpec((1, 2 * DN_HEADS, c), lambda i, j: (i, 0, j)),
                  _const_spec(cw.shape), _const_spec(alog2.shape), _const_spec(dtb2.shape),
                  _const_spec(alogt.shape), _const_spec(dtbt.shape)],
        out_specs=pl.BlockSpec((1, c, DN_WIDTH), lambda i, j: (i, j, 0)),
        out_shape=jax.ShapeDtypeStruct((b, s, DN_WIDTH), BF16),
        scratch_shapes=[pltpu.VMEM((c + CONV_HALO, 3 * DN_WIDTH), F32),
                        pltpu.VMEM((DN_HEADS, DN_HEAD_DIM, DN_HEAD_DIM), F32)],
        compiler_params=_params("parallel", "arbitrary"),
        name="deltanet",
    )(qkv, bd, bdt, cw, alog2, dtb2, alogt, dtbt)


def _rms(x, w):
    return x * lax.rsqrt(jnp.mean(x * x, axis=-1, keepdims=True) + EPS) * w


def _post_kernel(o_ref, z_ref, ga_ref, gb_ref, ya_ref, x_ref, dnw_ref, wdn_ref, wmix_ref, fnw_ref,
                 h1_ref, xn2t_ref):
    o = o_ref[...].astype(F32)
    z = z_ref[...].astype(F32)
    parts = []
    for h in range(DN_HEADS):
        sl = slice(h * DN_HEAD_DIM, (h + 1) * DN_HEAD_DIM)
        parts.append(_rms(o[:, sl], dnw_ref[...]))
    on = jnp.concatenate(parts, axis=-1) * (z * _sigmoid(z))
    yb = jnp.dot(on.astype(BF16), wdn_ref[...], preferred_element_type=F32)
    merged = (_sigmoid(ga_ref[...].astype(F32)) * ya_ref[...].astype(F32)
              + _sigmoid(gb_ref[...].astype(F32)) * yb)
    h1 = x_ref[...] + jnp.dot(merged.astype(BF16), wmix_ref[...], preferred_element_type=F32)
    h1_ref[...] = h1
    xn2t_ref[...] = _rms(h1, fnw_ref[...]).T.astype(BF16)


def _post(o, z, ga, gb, ya, x2, dnw, wdn, wmix, fnw):
    n = x2.shape[0]
    t = POST_TILE
    row = pl.BlockSpec((t, D_MODEL), lambda i: (i, 0))
    return pl.pallas_call(
        _post_kernel,
        grid=(n // t,),
        in_specs=[row, row, row, row, row, row, _const_spec(dnw.shape), _const_spec(wdn.shape),
                  _const_spec(wmix.shape), _const_spec(fnw.shape)],
        out_specs=[row, pl.BlockSpec((D_MODEL, t), lambda i: (0, i))],
        out_shape=[jax.ShapeDtypeStruct((n, D_MODEL), F32), jax.ShapeDtypeStruct((D_MODEL, n), BF16)],
        compiler_params=_params("parallel"),
        name="post",
    )(o, z, ga, gb, ya, x2, dnw, wdn, wmix, fnw)


_INT_MIN = -2 ** 31


def _order_key(x):
    b = lax.bitcast_convert_type(x, jnp.int32)
    return b ^ ((b >> 31) & 0x7FFFFFFF)


def _key_value(key):
    return lax.bitcast_convert_type(key ^ ((key >> 31) & 0x7FFFFFFF), F32)


def _extract_fast(x, rounds, vals_ref=None):
    ranks, bad = [], []
    for l0 in range(0, x.shape[1], V7X_LANES):
        lanes = slice(l0, l0 + V7X_LANES)
        key = _order_key(x[:, lanes])
        for r in range(rounds):
            m = jnp.max(key, axis=0, keepdims=True)
            if vals_ref is not None:
                vals_ref[r:r + 1, lanes] = _key_value(m)
            key = jnp.where(key == m, _INT_MIN + r, key)
        hit = key < _INT_MIN + rounds
        ranks.append(jnp.where(hit, key - _INT_MIN, rounds).astype(F32))
        bad.append(jnp.abs(jnp.sum(hit.astype(F32), axis=0, keepdims=True) - float(rounds)))
    ok = jnp.max(jnp.concatenate(bad, axis=1)) == 0.0
    return jnp.concatenate(ranks, axis=1), ok


def _extract_exact(work, rounds, vals_ref=None):
    rows = work.shape[0]
    iota = lax.broadcasted_iota(jnp.int32, work.shape, 0).astype(F32)
    rank = jnp.full(work.shape, float(rounds), F32)
    for r in range(rounds):
        m = jnp.max(work, axis=0, keepdims=True)
        idx = jnp.min(jnp.where(work == m, iota, float(rows)), axis=0, keepdims=True)
        hit = iota == idx
        rank = jnp.where(hit, float(r), rank)
        work = jnp.where(hit, -jnp.inf, work)
        if vals_ref is not None:
            vals_ref[r:r + 1, :] = m
    return rank


def _bcast_rows_bf16(row, rows):
    packed = jnp.broadcast_to(row, (BF16_SUBLANES, row.shape[1])).astype(BF16)
    return jnp.concatenate([packed] * (rows // BF16_SUBLANES), axis=0)


def _peer_kernel(xnt_ref, h1_ref, wq_ref, k1_ref, k2_ref, u_first_ref, u_odd_ref, u_even_ref,
                 vt_odd_ref, vt_even_ref, vt_last_ref, fw_ref, out_ref,
                 e1_ref, n1_ref, r2_ref, e2_ref, rk1_ref, rk2_ref, v1_ref, v2_ref, cand_ref, sel_ref, yt_ref,
                 h_even_ref, h_odd_ref, p_even_ref, p_odd_ref):
    ib = pl.program_id(1)
    nib = pl.num_programs(1)
    t = xnt_ref.shape[1]

    @pl.when(ib == 0)
    def _():
        yt_ref[...] = jnp.zeros(yt_ref.shape, F32)
        p_odd_ref[...] = jnp.zeros(p_odd_ref.shape, BF16)
        xnt = xnt_ref[...]
        h_even_ref[...] = jnp.dot(u_first_ref[...], xnt, preferred_element_type=F32)

        def head(h, carry):
            r0 = pl.multiple_of(h * 2 * PEER_HALF, 2 * PEER_HALF)
            q1 = jnp.dot(wq_ref[pl.ds(r0, PEER_HALF), :], xnt, preferred_element_type=F32)
            q2 = jnp.dot(wq_ref[pl.ds(r0 + PEER_HALF, PEER_HALF), :], xnt, preferred_element_type=F32)
            s1 = jnp.dot(k1_ref[h], q1.astype(BF16), preferred_element_type=F32)
            s2 = jnp.dot(k2_ref[h], q2.astype(BF16), preferred_element_type=F32)
            rank1, ok1 = _extract_fast(s1, PEER_TOPK, v1_ref)
            rank2, ok2 = _extract_fast(s2, PEER_TOPK, v2_ref)
            rk1_ref[...] = rank1
            rk2_ref[...] = rank2

            @pl.when(jnp.logical_not(ok1 & ok2))
            def _():
                rk1_ref[...] = _extract_exact(s1, PEER_TOPK, v1_ref)
                rk2_ref[...] = _extract_exact(s2, PEER_TOPK, v2_ref)

            cand_ref[...] = jnp.full(cand_ref.shape, -jnp.inf, F32)
            for n, (a, b) in enumerate(PEER_CANDS):
                cand_ref[n:n + 1, :] = v1_ref[a:a + 1, :] + v2_ref[b:b + 1, :]
            cand = cand_ref[...]
            crank, ok3 = _extract_fast(cand, PEER_TOPK)
            sel_ref[...] = crank

            @pl.when(jnp.logical_not(ok3))
            def _():
                sel_ref[...] = _extract_exact(cand, PEER_TOPK)

            sel = sel_ref[...] < float(PEER_TOPK)
            cexp = jnp.where(sel, jnp.exp(jnp.where(sel, cand - cand[0:1, :], 0.0)), 0.0)
            inv_z = 1.0 / jnp.sum(cexp, axis=0, keepdims=True)
            self32 = sel.astype(F32)
            rank1 = rk1_ref[...]
            n1 = jnp.zeros(rank1.shape, F32)
            n = 0
            for a in range(PEER_TOPK):
                width = PEER_TOPK // (a + 1)
                n_a = jnp.sum(self32[n:n + width, :], axis=0, keepdims=True)
                n1 = jnp.where(rank1 == float(a), n_a, n1)
                n += width
            e1_ref[h] = jnp.exp(s1 - v1_ref[0:1, :]) * (0.5 * inv_z)
            n1_ref[h] = n1
            r2_ref[h] = rk2_ref[...].astype(BF16)
            e2_ref[h] = jnp.exp(s2 - v2_ref[0:1, :]).astype(BF16)
            return carry

        lax.fori_loop(0, PEER_HEADS, head, 0)

    th = t // 2

    def stage(blk, hf, u_next_ref, h_next_ref, h_cur_ref, p_cur_ref, vt_prev_ref, p_prev_ref):
        lanes = slice(hf * th, (hf + 1) * th)
        h_next_ref[:, lanes] = jnp.dot(u_next_ref[...], xnt_ref[:, lanes], preferred_element_type=F32)
        for kk in range(PEER_IBLK):
            i = blk * PEER_IBLK + kk
            rows = slice(kk * N_KEYS, (kk + 1) * N_KEYS)
            hk = h_cur_ref[rows, lanes]
            act = hk * (1.0 + lax.erf(hk * 0.7071067811865476))
            g = jnp.zeros((N_KEYS, th), BF16)
            for h in range(PEER_HEADS):
                e1row = _bcast_rows_bf16(e1_ref[h, pl.ds(i, 1), lanes], N_KEYS)
                n1row = _bcast_rows_bf16(n1_ref[h, pl.ds(i, 1), lanes], N_KEYS)
                g = g + e1row * jnp.where(r2_ref[h, :, lanes] < n1row, e2_ref[h, :, lanes], jnp.zeros((), BF16))
            p_cur_ref[rows, lanes] = g * act.astype(BF16)
        yt_ref[:, lanes] += jnp.dot(vt_prev_ref[...], p_prev_ref[:, lanes], preferred_element_type=F32)

    for hf in range(2):
        stage(2 * ib, hf, u_odd_ref, h_odd_ref, h_even_ref, p_even_ref, vt_odd_ref, p_odd_ref)
    for hf in range(2):
        stage(2 * ib + 1, hf, u_even_ref, h_even_ref, h_odd_ref, p_odd_ref, vt_even_ref, p_even_ref)

    @pl.when(ib == nib - 1)
    def _():
        yt = yt_ref[...] + jnp.dot(vt_last_ref[...], p_odd_ref[...], preferred_element_type=F32)
        hfin = h1_ref[...] + yt.T
        out_ref[...] = _rms(hfin, fw_ref[...])


def _peer(xn2t, h1, wqt, k1, k2, u, vt, fw):
    n = xn2t.shape[1]
    t = PEER_TILE
    eb = PEER_IBLK * N_KEYS
    nblk = (N_KEYS * N_KEYS) // eb
    assert nblk % 2 == 0
    tok = pl.BlockSpec((t, D_MODEL), lambda i, j: (i, 0))
    big = pltpu.VMEM((PEER_HEADS, N_KEYS, t), F32)
    bigb = pltpu.VMEM((PEER_HEADS, N_KEYS, t), BF16)
    keys = pltpu.VMEM((N_KEYS, t), F32)
    u_spec = lambda f: pl.BlockSpec((eb, D_MODEL), lambda i, j: (f(j), 0))
    vt_spec = lambda f: pl.BlockSpec((D_MODEL, eb), lambda i, j: (0, f(j)))
    return pl.pallas_call(
        _peer_kernel,
        grid=(n // t, nblk // 2),
        in_specs=[pl.BlockSpec((D_MODEL, t), lambda i, j: (0, i)), tok,
                  _const_spec(wqt.shape), _const_spec(k1.shape), _const_spec(k2.shape),
                  u_spec(lambda j: 0), u_spec(lambda j: 2 * j + 1),
                  u_spec(lambda j: jnp.minimum(2 * j + 2, nblk - 1)),
                  vt_spec(lambda j: jnp.maximum(2 * j - 1, 0)), vt_spec(lambda j: 2 * j),
                  vt_spec(lambda j: nblk - 1),
                  _const_spec(fw.shape)],
        out_specs=tok,
        out_shape=jax.ShapeDtypeStruct((n, D_MODEL), F32),
        scratch_shapes=[big, big, bigb, bigb, keys, keys,
                        pltpu.VMEM((PEER_TOPK, t), F32), pltpu.VMEM((PEER_TOPK, t), F32),
                        pltpu.VMEM((PEER_CAND_ROWS, t), F32), pltpu.VMEM((PEER_CAND_ROWS, t), F32),
                        pltpu.VMEM((D_MODEL, t), F32),
                        pltpu.VMEM((eb, t), F32), pltpu.VMEM((eb, t), F32),
                        pltpu.VMEM((eb, t), BF16), pltpu.VMEM((eb, t), BF16)],
        compiler_params=_params("parallel", "arbitrary"),
        name="peer",
    )(xn2t, h1, wqt, k1, k2, u, u, u, vt, vt, vt, fw)


def kernel(x, mix_norm_w, w_in, pool_w, pool_scale, conv_w, a_log, dt_bias, dn_norm_w, w_pool_up,
           w_dn_up, w_mix_out, ffn_norm_w, peer_w_query, peer_keys_1, peer_keys_2, peer_down, peer_up,
           final_norm_w):
    b, s, d = x.shape
    n = b * s
    assert d == D_MODEL and s % POOL_TILE == 0 and s % DN_CHUNK == 0
    assert n % IN_TILE == 0 and n % POST_TILE == 0 and n % PEER_TILE == 0
    assert w_in.shape[0] == 1, "single-layer block"
    l = 0
    h = x.reshape(n, d)

    c0 = POOL_WIDTH
    c1 = c0 + 3 * DN_WIDTH
    c2 = c1 + DN_WIDTH
    c3 = c2 + 2 * DN_HEADS
    c4 = c3 + D_MODEL
    wi = w_in[l]
    xa, qkv, z, ga, gb, bd = _in_proj(
        h, mix_norm_w[l].reshape(1, d),
        wi[:, :c0].astype(BF16), wi[:, c0:c1].astype(BF16), wi[:, c1:c2].astype(BF16),
        wi[:, c3:c4].astype(BF16), wi[:, c4:].astype(BF16), wi[:, c2:c3])

    ya = _pool(xa.reshape(b, s, POOL_WIDTH), pool_w[l].astype(BF16), pool_scale[l].reshape(1, POOL_WIDTH),
               w_pool_up[l].astype(BF16))

    bd3 = bd.reshape(b, s, 2 * DN_HEADS)
    o = _deltanet(qkv.reshape(b, s, 3 * DN_WIDTH), bd3, bd3.transpose(0, 2, 1), conv_w[l], a_log[l],
                  dt_bias[l])

    h1, xn2t = _post(o.reshape(n, DN_WIDTH), z, ga, gb, ya.reshape(n, d), h,
                    dn_norm_w[l].reshape(1, DN_HEAD_DIM), w_dn_up[l].astype(BF16),
                    w_mix_out[l].astype(BF16), ffn_norm_w[l].reshape(1, d))

    out = _peer(xn2t, h1, peer_w_query[l].T.astype(BF16), peer_keys_1[l].astype(BF16),
                peer_keys_2[l].astype(BF16), peer_down[l].astype(BF16), peer_up[l].T.astype(BF16),
                final_norm_w.reshape(1, d))
    return out.reshape(b, s, d)
```

```python
import functools

import jax
import jax.numpy as jnp
from jax import lax
from jax.experimental import pallas as pl
from jax.experimental.pallas import tpu as pltpu

F32 = jnp.float32
BF16 = jnp.bfloat16
EPS = 1e-6

D_MODEL = 1024
POOL_WINDOWS = (2, 4, 8, 16)
POOL_GROUP_DIM = 128
POOL_WIDTH = 512
POOL_HALO = 16
DN_HEADS = 8
DN_HEAD_DIM = 128
DN_WIDTH = DN_HEADS * DN_HEAD_DIM
CONV_WIDTH = 4
CONV_HALO = 8
DN_CHUNK = 128
PEER_HEADS = 8
N_KEYS = 128
PEER_TOPK = 16
PEER_HALF = 128

V7X_LANES = 128
BF16_SUBLANES = 16
VMEM_LIMIT = 56 * 1024 * 1024

IN_TILE = 512
POOL_TILE = 512
POST_TILE = 256
PEER_TILE = 1024
PEER_PIECE = 256
PEER_IBLK = 4

PEER_CANDS = tuple((a, b) for a in range(PEER_TOPK) for b in range(PEER_TOPK)
                   if (a + 1) * (b + 1) <= PEER_TOPK)
PEER_CAND_ROWS = 56

_NT = (((1,), (1,)), ((), ()))
_TN = (((0,), (0,)), ((), ()))


def _params(*sem):
    return pltpu.CompilerParams(dimension_semantics=sem, vmem_limit_bytes=VMEM_LIMIT)


def _const_spec(shape):
    return pl.BlockSpec(shape, lambda *_: (0,) * len(shape))


def _resident_spec(shape):
    return pl.BlockSpec(shape, lambda *_: (0,) * len(shape), pipeline_mode=pl.Buffered(1))


def _sigmoid(x):
    return 1.0 / (1.0 + jnp.exp(-x))


def _in_proj_kernel(x_ref, nw_ref, wxa_ref, wqkv_ref, wz_ref, wga_ref, wgb_ref, wbd_ref,
                    xa_ref, qkv_ref, z_ref, ga_ref, gb_ref, bd_ref):
    x = x_ref[...]
    xn = x * lax.rsqrt(jnp.mean(x * x, axis=-1, keepdims=True) + EPS) * nw_ref[...]
    xb = xn.astype(BF16)
    xa_ref[...] = jnp.dot(xb, wxa_ref[...], preferred_element_type=F32)
    qkv_ref[...] = jnp.dot(xb, wqkv_ref[...], preferred_element_type=F32)
    z_ref[...] = jnp.dot(xb, wz_ref[...], preferred_element_type=F32).astype(BF16)
    ga_ref[...] = jnp.dot(xb, wga_ref[...], preferred_element_type=F32).astype(BF16)
    gb_ref[...] = jnp.dot(xb, wgb_ref[...], preferred_element_type=F32).astype(BF16)
    bd_ref[...] = jnp.dot(xn, wbd_ref[...], preferred_element_type=F32,
                          precision=lax.Precision.HIGHEST)


def _in_proj(x2, nw, wxa, wqkv, wz, wga, wgb, wbd):
    n = x2.shape[0]
    t = IN_TILE
    row = lambda w: pl.BlockSpec((t, w), lambda i: (i, 0))
    widths = (POOL_WIDTH, 3 * DN_WIDTH, DN_WIDTH, D_MODEL, D_MODEL, 2 * DN_HEADS)
    dtypes = (F32, F32, BF16, BF16, BF16, F32)
    return pl.pallas_call(
        _in_proj_kernel,
        grid=(n // t,),
        in_specs=[row(D_MODEL), _const_spec(nw.shape), _resident_spec(wxa.shape), _resident_spec(wqkv.shape),
                  _resident_spec(wz.shape), _resident_spec(wga.shape), _resident_spec(wgb.shape),
                  _const_spec(wbd.shape)],
        out_specs=[row(w) for w in widths],
        out_shape=[jax.ShapeDtypeStruct((n, w), dt) for w, dt in zip(widths, dtypes)],
        compiler_params=_params("parallel"),
        name="in_proj",
    )(x2, nw, wxa, wqkv, wz, wga, wgb, wbd)


def _pool_kernel(xa_ref, pw_ref, ps_ref, wup_ref, ya_ref, buf_ref):
    s = pl.program_id(1)
    t = xa_ref.shape[1]

    @pl.when(s == 0)
    def _():
        buf_ref[0:POOL_HALO, :] = jnp.zeros((POOL_HALO, POOL_WIDTH), F32)

    @pl.when(s > 0)
    def _():
        buf_ref[0:POOL_HALO, :] = buf_ref[t:t + POOL_HALO, :]

    buf_ref[POOL_HALO:, :] = xa_ref[0]
    pos = s * t + lax.broadcasted_iota(jnp.int32, (t, 1), 0)
    ys = []
    for g, win in enumerate(POOL_WINDOWS):
        sl = slice(g * POOL_GROUP_DIM, (g + 1) * POOL_GROUP_DIM)
        xg = buf_ref[POOL_HALO:POOL_HALO + t, sl]
        acc = xg
        for k in range(1, win):
            acc = acc + buf_ref[POOL_HALO - k:POOL_HALO - k + t, sl]
        cnt = jnp.minimum(pos + 1, win).astype(F32)
        pooled = acc / cnt - xg
        y = jnp.dot(pooled.astype(BF16), pw_ref[g], preferred_element_type=F32)
        ys.append(y * ps_ref[:, sl])
    y = jnp.concatenate(ys, axis=-1)
    ya_ref[0] = jnp.dot(y.astype(BF16), wup_ref[...], preferred_element_type=F32).astype(BF16)


def _pool(xa, pw, ps, wup):
    b, s, _ = xa.shape
    t = POOL_TILE
    return pl.pallas_call(
        _pool_kernel,
        grid=(b, s // t),
        in_specs=[pl.BlockSpec((1, t, POOL_WIDTH), lambda i, j: (i, j, 0)),
                  _const_spec(pw.shape), _const_spec(ps.shape), _const_spec(wup.shape)],
        out_specs=pl.BlockSpec((1, t, D_MODEL), lambda i, j: (i, j, 0)),
        out_shape=jax.ShapeDtypeStruct((b, s, D_MODEL), BF16),
        scratch_shapes=[pltpu.VMEM((t + POOL_HALO, POOL_WIDTH), F32)],
        compiler_params=_params("parallel", "arbitrary"),
        name="pool",
    )(xa, pw, ps, wup)


def _softplus(x):
    return jnp.maximum(x, 0.0) + jnp.log1p(jnp.exp(-jnp.abs(x)))


def _dn_kernel(qkv_ref, bd_ref, bdt_ref, cw_ref, alog_ref, dtb_ref, alogt_ref, dtbt_ref,
               o_ref, cbuf_ref, st_ref):
    s = pl.program_id(1)
    c = DN_CHUNK
    hd = DN_HEAD_DIM

    @pl.when(s == 0)
    def _():
        cbuf_ref[0:CONV_HALO, :] = jnp.zeros((CONV_HALO, 3 * DN_WIDTH), F32)
        st_ref[...] = jnp.zeros(st_ref.shape, F32)

    @pl.when(s > 0)
    def _():
        cbuf_ref[0:CONV_HALO, :] = cbuf_ref[c:c + CONV_HALO, :]

    cbuf_ref[CONV_HALO:, :] = qkv_ref[0]

    bd = bd_ref[0]
    bdt = bdt_ref[0]
    beta_all = _sigmoid(bd[:, 0:DN_HEADS])
    g_all = -jnp.exp(alog_ref[...]) * _softplus(bd[:, DN_HEADS:] + dtb_ref[...])
    g_allt = -jnp.exp(alogt_ref[...]) * _softplus(bdt[DN_HEADS:, :] + dtbt_ref[...])

    row = lax.broadcasted_iota(jnp.int32, (c, c), 0)
    col = lax.broadcasted_iota(jnp.int32, (c, c), 1)
    tril = row >= col
    strict = row > col
    gc = jnp.dot(tril.astype(F32), g_all, preferred_element_type=F32, precision=lax.Precision.HIGHEST)
    gct = jnp.dot(g_allt, (row <= col).astype(F32), preferred_element_type=F32,
                  precision=lax.Precision.HIGHEST)
    eye = (row == col).astype(F32)

    def conv_silu(lane0):
        sl = slice(lane0, lane0 + hd)
        y = cw_ref[0:1, sl] * cbuf_ref[CONV_HALO - 3:CONV_HALO - 3 + c, sl]
        for j in range(1, CONV_WIDTH):
            y = y + cw_ref[j:j + 1, sl] * cbuf_ref[CONV_HALO - 3 + j:CONV_HALO - 3 + j + c, sl]
        return y * _sigmoid(y)

    def l2n(v):
        return v * lax.rsqrt(jnp.sum(v * v, axis=-1, keepdims=True) + EPS)

    hs = range(DN_HEADS)
    q = [l2n(conv_silu(h * hd)) * (hd ** -0.5) for h in hs]
    k = [l2n(conv_silu(DN_WIDTH + h * hd)) for h in hs]
    v = [conv_silu(2 * DN_WIDTH + h * hd) for h in hs]
    gcol = [gc[:, h:h + 1] for h in hs]
    grow = [gct[h:h + 1, :] for h in hs]
    decay = [jnp.where(tril, jnp.exp(jnp.where(tril, gcol[h] - grow[h], 0.0)), 0.0) for h in hs]
    kb = [k[h] * beta_all[:, h:h + 1] for h in hs]
    vb = [v[h] * beta_all[:, h:h + 1] for h in hs]
    kf = [k[h].astype(BF16) for h in hs]
    lower = [jnp.where(strict, lax.dot_general(kb[h].astype(BF16), kf[h], _NT, preferred_element_type=F32)
                       * decay[h], 0.0) for h in hs]
    attn = [(lax.dot_general(q[h].astype(BF16), kf[h], _NT, preferred_element_type=F32) * decay[h]).astype(BF16)
            for h in hs]
    pair = ((row ^ col) == 1) & strict
    tinv = [eye - jnp.where(pair, lower[h], 0.0) for h in hs]
    blk = 2
    while blk < c:
        m = ((row ^ col) < 2 * blk) & ((row & blk) != 0) & ((col & blk) == 0)
        tb = [tinv[h].astype(BF16) for h in hs]
        left = [jnp.dot(tb[h], jnp.where(m, lower[h], 0.0).astype(BF16), preferred_element_type=F32).astype(BF16)
                for h in hs]
        tinv = [tinv[h] - jnp.dot(left[h], tb[h], preferred_element_type=F32) for h in hs]
        blk *= 2
    eg = [jnp.exp(gcol[h]) for h in hs]
    uw = [jnp.dot(tinv[h].astype(BF16),
                  jnp.concatenate([vb[h], kb[h] * eg[h]], axis=-1).astype(BF16), preferred_element_type=F32)
          for h in hs]
    st = [st_ref[h] for h in hs]
    ws = [jnp.dot(jnp.concatenate([uw[h][:, hd:], q[h] * eg[h]], axis=0).astype(BF16), st[h].astype(BF16),
                  preferred_element_type=F32) for h in hs]
    vnb = [(uw[h][:, :hd] - ws[h][:c]).astype(BF16) for h in hs]
    o = [ws[h][c:] + jnp.dot(attn[h], vnb[h], preferred_element_type=F32) for h in hs]
    for h in hs:
        g_last = grow[h][:, c - 1:c]
        k_dec = (k[h] * jnp.exp(g_last - gcol[h])).astype(BF16)
        st_ref[h] = st[h] * jnp.exp(g_last) + lax.dot_general(k_dec, vnb[h], _TN, preferred_element_type=F32)
    o_ref[0] = jnp.concatenate(o, axis=-1).astype(BF16)


def _deltanet(qkv, bd, bdt, cw, alog, dtb):
    b, s, _ = qkv.shape
    c = DN_CHUNK
    alog2, dtb2 = alog.reshape(1, DN_HEADS), dtb.reshape(1, DN_HEADS)
    alogt, dtbt = alog.reshape(DN_HEADS, 1), dtb.reshape(DN_HEADS, 1)
    return pl.pallas_call(
        _dn_kernel,
        grid=(b, s // c),
        in_specs=[pl.BlockSpec((1, c, 3 * DN_WIDTH), lambda i, j: (i, j, 0)),
                  pl.BlockSpec((1, c, 2 * DN_HEADS), lambda i, j: (i, j, 0)),
                  pl.BlockSpec((1, 2 * DN_HEADS, c), lambda i, j: (i, 0, j)),
                  _const_spec(cw.shape), _const_spec(alog2.shape), _const_spec(dtb2.shape),
                  _const_spec(alogt.shape), _const_spec(dtbt.shape)],
        out_specs=pl.BlockSpec((1, c, DN_WIDTH), lambda i, j: (i, j, 0)),
        out_shape=jax.ShapeDtypeStruct((b, s, DN_WIDTH), BF16),
        scratch_shapes=[pltpu.VMEM((c + CONV_HALO, 3 * DN_WIDTH), F32),
                        pltpu.VMEM((DN_HEADS, DN_HEAD_DIM, DN_HEAD_DIM), F32)],
        compiler_params=_params("parallel", "arbitrary"),
        name="deltanet",
    )(qkv, bd, bdt, cw, alog2, dtb2, alogt, dtbt)


def _rms(x, w):
    return x * lax.rsqrt(jnp.mean(x * x, axis=-1, keepdims=True) + EPS) * w


def _post_kernel(o_ref, z_ref, ga_ref, gb_ref, ya_ref, x_ref, dnw_ref, wdn_ref, wmix_ref, fnw_ref,
                 h1_ref, xn2t_ref):
    o = o_ref[...].astype(F32)
    z = z_ref[...].astype(F32)
    parts = []
    for h in range(DN_HEADS):
        sl = slice(h * DN_HEAD_DIM, (h + 1) * DN_HEAD_DIM)
        parts.append(_rms(o[:, sl], dnw_ref[...]))
    on = jnp.concatenate(parts, axis=-1) * (z * _sigmoid(z))
    yb = jnp.dot(on.astype(BF16), wdn_ref[...], preferred_element_type=F32)
    merged = (_sigmoid(ga_ref[...].astype(F32)) * ya_ref[...].astype(F32)
              + _sigmoid(gb_ref[...].astype(F32)) * yb)
    h1 = x_ref[...] + jnp.dot(merged.astype(BF16), wmix_ref[...], preferred_element_type=F32)
    h1_ref[...] = h1
    xn2t_ref[...] = _rms(h1, fnw_ref[...]).T.astype(BF16)


def _post(o, z, ga, gb, ya, x2, dnw, wdn, wmix, fnw):
    n = x2.shape[0]
    t = POST_TILE
    row = pl.BlockSpec((t, D_MODEL), lambda i: (i, 0))
    return pl.pallas_call(
        _post_kernel,
        grid=(n // t,),
        in_specs=[row, row, row, row, row, row, _const_spec(dnw.shape), _const_spec(wdn.shape),
                  _const_spec(wmix.shape), _const_spec(fnw.shape)],
        out_specs=[row, pl.BlockSpec((D_MODEL, t), lambda i: (0, i))],
        out_shape=[jax.ShapeDtypeStruct((n, D_MODEL), F32), jax.ShapeDtypeStruct((D_MODEL, n), BF16)],
        compiler_params=_params("parallel"),
        name="post",
    )(o, z, ga, gb, ya, x2, dnw, wdn, wmix, fnw)


_INT_MIN = -2 ** 31


def _order_key(x):
    b = lax.bitcast_convert_type(x, jnp.int32)
    return b ^ ((b >> 31) & 0x7FFFFFFF)


def _key_value(key):
    return lax.bitcast_convert_type(key ^ ((key >> 31) & 0x7FFFFFFF), F32)


def _extract_fast(x, rounds, vals_ref=None):
    ranks, bad = [], []
    for l0 in range(0, x.shape[1], V7X_LANES):
        lanes = slice(l0, l0 + V7X_LANES)
        key = _order_key(x[:, lanes])
        for r in range(rounds):
            m = jnp.max(key, axis=0, keepdims=True)
            if vals_ref is not None:
                vals_ref[r:r + 1, lanes] = _key_value(m)
            key = jnp.where(key == m, _INT_MIN + r, key)
        hit = key < _INT_MIN + rounds
        ranks.append(jnp.where(hit, key - _INT_MIN, rounds).astype(F32))
        bad.append(jnp.abs(jnp.sum(hit.astype(F32), axis=0, keepdims=True) - float(rounds)))
    ok = jnp.max(jnp.concatenate(bad, axis=1)) == 0.0
    return jnp.concatenate(ranks, axis=1), ok


def _extract_exact(work, rounds, vals_ref=None):
    rows = work.shape[0]
    iota = lax.broadcasted_iota(jnp.int32, work.shape, 0).astype(F32)
    rank = jnp.full(work.shape, float(rounds), F32)
    for r in range(rounds):
        m = jnp.max(work, axis=0, keepdims=True)
        idx = jnp.min(jnp.where(work == m, iota, float(rows)), axis=0, keepdims=True)
        hit = iota == idx
        rank = jnp.where(hit, float(r), rank)
        work = jnp.where(hit, -jnp.inf, work)
        if vals_ref is not None:
            vals_ref[r:r + 1, :] = m
    return rank


def _bcast_rows_bf16(row, rows):
    packed = jnp.broadcast_to(row, (BF16_SUBLANES, row.shape[1])).astype(BF16)
    return jnp.concatenate([packed] * (rows // BF16_SUBLANES), axis=0)


def _peer_kernel(xnt_ref, h1_ref, wq_ref, k1_ref, k2_ref, u_first_ref, u_odd_ref, u_even_ref,
                 vt_odd_ref, vt_even_ref, vt_last_ref, fw_ref, out_ref,
                 e1_ref, n1_ref, r2_ref, e2_ref, rk1_ref, rk2_ref, v1_ref, v2_ref, cand_ref, sel_ref, yt_ref,
                 h_even_ref, h_odd_ref, p_even_ref, p_odd_ref):
    ib = pl.program_id(1)
    nib = pl.num_programs(1)
    t = xnt_ref.shape[1]

    @pl.when(ib == 0)
    def _():
        yt_ref[...] = jnp.zeros(yt_ref.shape, F32)
        p_odd_ref[...] = jnp.zeros(p_odd_ref.shape, BF16)
        xnt = xnt_ref[...]
        h_even_ref[...] = jnp.dot(u_first_ref[...], xnt, preferred_element_type=F32)

        def head(h, carry):
            r0 = pl.multiple_of(h * 2 * PEER_HALF, 2 * PEER_HALF)
            q1 = jnp.dot(wq_ref[pl.ds(r0, PEER_HALF), :], xnt, preferred_element_type=F32)
            q2 = jnp.dot(wq_ref[pl.ds(r0 + PEER_HALF, PEER_HALF), :], xnt, preferred_element_type=F32)
            s1 = jnp.dot(k1_ref[h], q1.astype(BF16), preferred_element_type=F32)
            s2 = jnp.dot(k2_ref[h], q2.astype(BF16), preferred_element_type=F32)
            rank1, ok1 = _extract_fast(s1, PEER_TOPK, v1_ref)
            rank2, ok2 = _extract_fast(s2, PEER_TOPK, v2_ref)
            rk1_ref[...] = rank1
            rk2_ref[...] = rank2

            @pl.when(jnp.logical_not(ok1 & ok2))
            def _():
                rk1_ref[...] = _extract_exact(s1, PEER_TOPK, v1_ref)
                rk2_ref[...] = _extract_exact(s2, PEER_TOPK, v2_ref)

            cand_ref[...] = jnp.full(cand_ref.shape, -jnp.inf, F32)
            for n, (a, b) in enumerate(PEER_CANDS):
                cand_ref[n:n + 1, :] = v1_ref[a:a + 1, :] + v2_ref[b:b + 1, :]
            cand = cand_ref[...]
            crank, ok3 = _extract_fast(cand, PEER_TOPK)
            sel_ref[...] = crank

            @pl.when(jnp.logical_not(ok3))
            def _():
                sel_ref[...] = _extract_exact(cand, PEER_TOPK)

            sel = sel_ref[...] < float(PEER_TOPK)
            cexp = jnp.where(sel, jnp.exp(jnp.where(sel, cand - cand[0:1, :], 0.0)), 0.0)
            inv_z = 1.0 / jnp.sum(cexp, axis=0, keepdims=True)
            self32 = sel.astype(F32)
            rank1 = rk1_ref[...]
            n1 = jnp.zeros(rank1.shape, F32)
            n = 0
            for a in range(PEER_TOPK):
                width = PEER_TOPK // (a + 1)
                n_a = jnp.sum(self32[n:n + width, :], axis=0, keepdims=True)
                n1 = jnp.where(rank1 == float(a), n_a, n1)
                n += width
            e1_ref[h] = jnp.exp(s1 - v1_ref[0:1, :]) * (0.5 * inv_z)
            n1_ref[h] = n1
            r2_ref[h] = rk2_ref[...].astype(BF16)
            e2_ref[h] = jnp.exp(s2 - v2_ref[0:1, :]).astype(BF16)
            return carry

        lax.fori_loop(0, PEER_HEADS, head, 0)

    th = PEER_PIECE

    def stage(blk, hf, u_next_ref, h_next_ref, h_cur_ref, p_cur_ref, vt_prev_ref, p_prev_ref):
        lanes = slice(hf * th, (hf + 1) * th)
        h_next_ref[:, lanes] = jnp.dot(u_next_ref[...], xnt_ref[:, lanes], preferred_element_type=F32)
        for kk in range(PEER_IBLK):
            i = blk * PEER_IBLK + kk
            rows = slice(kk * N_KEYS, (kk + 1) * N_KEYS)
            hk = h_cur_ref[rows, lanes]
            act = hk * (1.0 + lax.erf(hk * 0.7071067811865476))
            g = jnp.zeros((N_KEYS, th), BF16)
            for h in range(PEER_HEADS):
                e1row = _bcast_rows_bf16(e1_ref[h, pl.ds(i, 1), lanes], N_KEYS)
                n1row = _bcast_rows_bf16(n1_ref[h, pl.ds(i, 1), lanes], N_KEYS)
                g = g + e1row * jnp.where(r2_ref[h, :, lanes] < n1row, e2_ref[h, :, lanes], jnp.zeros((), BF16))
            p_cur_ref[rows, lanes] = g * act.astype(BF16)
        yt_ref[:, lanes] += jnp.dot(vt_prev_ref[...], p_prev_ref[:, lanes], preferred_element_type=F32)

    for hf in range(t // th):
        stage(2 * ib, hf, u_odd_ref, h_odd_ref, h_even_ref, p_even_ref, vt_odd_ref, p_odd_ref)
    for hf in range(t // th):
        stage(2 * ib + 1, hf, u_even_ref, h_even_ref, h_odd_ref, p_odd_ref, vt_even_ref, p_even_ref)

    @pl.when(ib == nib - 1)
    def _():
        yt = yt_ref[...] + jnp.dot(vt_last_ref[...], p_odd_ref[...], preferred_element_type=F32)
        hfin = h1_ref[...] + yt.T
        out_ref[...] = _rms(hfin, fw_ref[...])


def _peer(xn2t, h1, wqt, k1, k2, u, vt, fw):
    n = xn2t.shape[1]
    t = PEER_TILE
    eb = PEER_IBLK * N_KEYS
    nblk = (N_KEYS * N_KEYS) // eb
    assert nblk % 2 == 0
    tok = pl.BlockSpec((t, D_MODEL), lambda i, j: (i, 0), pipeline_mode=pl.Buffered(1))
    big = pltpu.VMEM((PEER_HEADS, N_KEYS, t), F32)
    bigb = pltpu.VMEM((PEER_HEADS, N_KEYS, t), BF16)
    keys = pltpu.VMEM((N_KEYS, t), F32)
    u_spec = lambda f: pl.BlockSpec((eb, D_MODEL), lambda i, j: (f(j), 0))
    vt_spec = lambda f: pl.BlockSpec((D_MODEL, eb), lambda i, j: (0, f(j)))
    return pl.pallas_call(
        _peer_kernel,
        grid=(n // t, nblk // 2),
        in_specs=[pl.BlockSpec((D_MODEL, t), lambda i, j: (0, i), pipeline_mode=pl.Buffered(1)), tok,
                  _resident_spec(wqt.shape), _const_spec(k1.shape), _const_spec(k2.shape),
                  u_spec(lambda j: 0), u_spec(lambda j: 2 * j + 1),
                  u_spec(lambda j: jnp.minimum(2 * j + 2, nblk - 1)),
                  vt_spec(lambda j: jnp.maximum(2 * j - 1, 0)), vt_spec(lambda j: 2 * j),
                  vt_spec(lambda j: nblk - 1),
                  _const_spec(fw.shape)],
        out_specs=tok,
        out_shape=jax.ShapeDtypeStruct((n, D_MODEL), F32),
        scratch_shapes=[big, big, bigb, bigb, keys, keys,
                        pltpu.VMEM((PEER_TOPK, t), F32), pltpu.VMEM((PEER_TOPK, t), F32),
                        pltpu.VMEM((PEER_CAND_ROWS, t), F32), pltpu.VMEM((PEER_CAND_ROWS, t), F32),
                        pltpu.VMEM((D_MODEL, t), F32),
                        pltpu.VMEM((eb, t), F32), pltpu.VMEM((eb, t), F32),
                        pltpu.VMEM((eb, t), BF16), pltpu.VMEM((eb, t), BF16)],
        compiler_params=_params("parallel", "arbitrary"),
        name="peer",
    )(xn2t, h1, wqt, k1, k2, u, u, u, vt, vt, vt, fw)


def kernel(x, mix_norm_w, w_in, pool_w, pool_scale, conv_w, a_log, dt_bias, dn_norm_w, w_pool_up,
           w_dn_up, w_mix_out, ffn_norm_w, peer_w_query, peer_keys_1, peer_keys_2, peer_down, peer_up,
           final_norm_w):
    b, s, d = x.shape
    n = b * s
    assert d == D_MODEL and s % POOL_TILE == 0 and s % DN_CHUNK == 0
    assert n % IN_TILE == 0 and n % POST_TILE == 0 and n % PEER_TILE == 0
    assert w_in.shape[0] == 1, "single-layer block"
    l = 0
    h = x.reshape(n, d)

    c0 = POOL_WIDTH
    c1 = c0 + 3 * DN_WIDTH
    c2 = c1 + DN_WIDTH
    c3 = c2 + 2 * DN_HEADS
    c4 = c3 + D_MODEL
    wi = w_in[l]
    xa, qkv, z, ga, gb, bd = _in_proj(
        h, mix_norm_w[l].reshape(1, d),
        wi[:, :c0].astype(BF16), wi[:, c0:c1].astype(BF16), wi[:, c1:c2].astype(BF16),
        wi[:, c3:c4].astype(BF16), wi[:, c4:].astype(BF16), wi[:, c2:c3])

    ya = _pool(xa.reshape(b, s, POOL_WIDTH), pool_w[l].astype(BF16), pool_scale[l].reshape(1, POOL_WIDTH),
               w_pool_up[l].astype(BF16))

    bd3 = bd.reshape(b, s, 2 * DN_HEADS)
    o = _deltanet(qkv.reshape(b, s, 3 * DN_WIDTH), bd3, bd3.transpose(0, 2, 1), conv_w[l], a_log[l],
                  dt_bias[l])

    h1, xn2t = _post(o.reshape(n, DN_WIDTH), z, ga, gb, ya.reshape(n, d), h,
                    dn_norm_w[l].reshape(1, DN_HEAD_DIM), w_dn_up[l].astype(BF16),
                    w_mix_out[l].astype(BF16), ffn_norm_w[l].reshape(1, d))

    out = _peer(xn2t, h1, peer_w_query[l].T.astype(BF16), peer_keys_1[l].astype(BF16),
                peer_keys_2[l].astype(BF16), peer_down[l].astype(BF16), peer_up[l].T.astype(BF16),
                final_norm_w.reshape(1, d))
    return out.reshape(b, s, d)
```

```python
import functools

import jax
import jax.numpy as jnp
from jax import lax
from jax.experimental import pallas as pl
from jax.experimental.pallas import tpu as pltpu

F32 = jnp.float32
BF16 = jnp.bfloat16
EPS = 1e-6

D_MODEL = 1024
POOL_WINDOWS = (2, 4, 8, 16)
POOL_GROUP_DIM = 128
POOL_WIDTH = 512
POOL_HALO = 16
DN_HEADS = 8
DN_HEAD_DIM = 128
DN_WIDTH = DN_HEADS * DN_HEAD_DIM
CONV_WIDTH = 4
CONV_HALO = 8
DN_CHUNK = 128
PEER_HEADS = 8
N_KEYS = 128
PEER_TOPK = 16
PEER_HALF = 128

V7X_LANES = 128
BF16_SUBLANES = 16
VMEM_LIMIT = 56 * 1024 * 1024

IN_TILE = 512
POOL_TILE = 512
POST_TILE = 256
PEER_TILE = 512
PEER_IBLK = 4

PEER_CANDS = tuple((a, b) for a in range(PEER_TOPK) for b in range(PEER_TOPK)
                   if (a + 1) * (b + 1) <= PEER_TOPK)
PEER_CAND_ROWS = 56

_NT = (((1,), (1,)), ((), ()))
_TN = (((0,), (0,)), ((), ()))


def _params(*sem):
    return pltpu.CompilerParams(dimension_semantics=sem, vmem_limit_bytes=VMEM_LIMIT)


def _const_spec(shape):
    return pl.BlockSpec(shape, lambda *_: (0,) * len(shape))


def _resident_spec(shape):
    return pl.BlockSpec(shape, lambda *_: (0,) * len(shape), pipeline_mode=pl.Buffered(1))


def _sigmoid(x):
    return 1.0 / (1.0 + jnp.exp(-x))


def _in_proj_kernel(x_ref, nw_ref, wxa_ref, wqkv_ref, wz_ref, wga_ref, wgb_ref, wbd_ref,
                    xa_ref, qkv_ref, z_ref, ga_ref, gb_ref, bd_ref):
    x = x_ref[...]
    xn = x * lax.rsqrt(jnp.mean(x * x, axis=-1, keepdims=True) + EPS) * nw_ref[...]
    xb = xn.astype(BF16)
    xa_ref[...] = jnp.dot(xb, wxa_ref[...], preferred_element_type=F32)
    qkv_ref[...] = jnp.dot(xb, wqkv_ref[...], preferred_element_type=F32)
    z_ref[...] = jnp.dot(xb, wz_ref[...], preferred_element_type=F32).astype(BF16)
    ga_ref[...] = jnp.dot(xb, wga_ref[...], preferred_element_type=F32).astype(BF16)
    gb_ref[...] = jnp.dot(xb, wgb_ref[...], preferred_element_type=F32).astype(BF16)
    bd_ref[...] = jnp.dot(xn, wbd_ref[...], preferred_element_type=F32,
                          precision=lax.Precision.HIGHEST)


def _in_proj(x2, nw, wxa, wqkv, wz, wga, wgb, wbd):
    n = x2.shape[0]
    t = IN_TILE
    row = lambda w: pl.BlockSpec((t, w), lambda i: (i, 0))
    widths = (POOL_WIDTH, 3 * DN_WIDTH, DN_WIDTH, D_MODEL, D_MODEL, 2 * DN_HEADS)
    dtypes = (F32, F32, BF16, BF16, BF16, F32)
    return pl.pallas_call(
        _in_proj_kernel,
        grid=(n // t,),
        in_specs=[row(D_MODEL), _const_spec(nw.shape), _resident_spec(wxa.shape), _resident_spec(wqkv.shape),
                  _resident_spec(wz.shape), _resident_spec(wga.shape), _resident_spec(wgb.shape),
                  _const_spec(wbd.shape)],
        out_specs=[row(w) for w in widths],
        out_shape=[jax.ShapeDtypeStruct((n, w), dt) for w, dt in zip(widths, dtypes)],
        compiler_params=_params("parallel"),
        name="in_proj",
    )(x2, nw, wxa, wqkv, wz, wga, wgb, wbd)


def _pool_kernel(xa_ref, pw_ref, ps_ref, wup_ref, ya_ref, buf_ref):
    s = pl.program_id(1)
    t = xa_ref.shape[1]

    @pl.when(s == 0)
    def _():
        buf_ref[0:POOL_HALO, :] = jnp.zeros((POOL_HALO, POOL_WIDTH), F32)

    @pl.when(s > 0)
    def _():
        buf_ref[0:POOL_HALO, :] = buf_ref[t:t + POOL_HALO, :]

    buf_ref[POOL_HALO:, :] = xa_ref[0]
    pos = s * t + lax.broadcasted_iota(jnp.int32, (t, 1), 0)
    ys = []
    for g, win in enumerate(POOL_WINDOWS):
        sl = slice(g * POOL_GROUP_DIM, (g + 1) * POOL_GROUP_DIM)
        xg = buf_ref[POOL_HALO:POOL_HALO + t, sl]
        acc = xg
        for k in range(1, win):
            acc = acc + buf_ref[POOL_HALO - k:POOL_HALO - k + t, sl]
        cnt = jnp.minimum(pos + 1, win).astype(F32)
        pooled = acc / cnt - xg
        y = jnp.dot(pooled.astype(BF16), pw_ref[g], preferred_element_type=F32)
        ys.append(y * ps_ref[:, sl])
    y = jnp.concatenate(ys, axis=-1)
    ya_ref[0] = jnp.dot(y.astype(BF16), wup_ref[...], preferred_element_type=F32).astype(BF16)


def _pool(xa, pw, ps, wup):
    b, s, _ = xa.shape
    t = POOL_TILE
    return pl.pallas_call(
        _pool_kernel,
        grid=(b, s // t),
        in_specs=[pl.BlockSpec((1, t, POOL_WIDTH), lambda i, j: (i, j, 0)),
                  _const_spec(pw.shape), _const_spec(ps.shape), _const_spec(wup.shape)],
        out_specs=pl.BlockSpec((1, t, D_MODEL), lambda i, j: (i, j, 0)),
        out_shape=jax.ShapeDtypeStruct((b, s, D_MODEL), BF16),
        scratch_shapes=[pltpu.VMEM((t + POOL_HALO, POOL_WIDTH), F32)],
        compiler_params=_params("parallel", "arbitrary"),
        name="pool",
    )(xa, pw, ps, wup)


def _softplus(x):
    return jnp.maximum(x, 0.0) + jnp.log1p(jnp.exp(-jnp.abs(x)))


def _dn_kernel(qkv_ref, bd_ref, bdt_ref, cw_ref, alog_ref, dtb_ref, alogt_ref, dtbt_ref,
               o_ref, cbuf_ref, st_ref):
    s = pl.program_id(1)
    c = DN_CHUNK
    hd = DN_HEAD_DIM

    @pl.when(s == 0)
    def _():
        cbuf_ref[0:CONV_HALO, :] = jnp.zeros((CONV_HALO, 3 * DN_WIDTH), F32)
        st_ref[...] = jnp.zeros(st_ref.shape, F32)

    @pl.when(s > 0)
    def _():
        cbuf_ref[0:CONV_HALO, :] = cbuf_ref[c:c + CONV_HALO, :]

    cbuf_ref[CONV_HALO:, :] = qkv_ref[0]

    bd = bd_ref[0]
    bdt = bdt_ref[0]
    beta_all = _sigmoid(bd[:, 0:DN_HEADS])
    g_all = -jnp.exp(alog_ref[...]) * _softplus(bd[:, DN_HEADS:] + dtb_ref[...])
    g_allt = -jnp.exp(alogt_ref[...]) * _softplus(bdt[DN_HEADS:, :] + dtbt_ref[...])

    row = lax.broadcasted_iota(jnp.int32, (c, c), 0)
    col = lax.broadcasted_iota(jnp.int32, (c, c), 1)
    tril = row >= col
    strict = row > col
    gc = jnp.dot(tril.astype(F32), g_all, preferred_element_type=F32, precision=lax.Precision.HIGHEST)
    gct = jnp.dot(g_allt, (row <= col).astype(F32), preferred_element_type=F32,
                  precision=lax.Precision.HIGHEST)
    eye = (row == col).astype(F32)

    def conv_silu(lane0):
        sl = slice(lane0, lane0 + hd)
        y = cw_ref[0:1, sl] * cbuf_ref[CONV_HALO - 3:CONV_HALO - 3 + c, sl]
        for j in range(1, CONV_WIDTH):
            y = y + cw_ref[j:j + 1, sl] * cbuf_ref[CONV_HALO - 3 + j:CONV_HALO - 3 + j + c, sl]
        return y * _sigmoid(y)

    def l2n(v):
        return v * lax.rsqrt(jnp.sum(v * v, axis=-1, keepdims=True) + EPS)

    hs = range(DN_HEADS)
    q = [l2n(conv_silu(h * hd)) * (hd ** -0.5) for h in hs]
    k = [l2n(conv_silu(DN_WIDTH + h * hd)) for h in hs]
    v = [conv_silu(2 * DN_WIDTH + h * hd) for h in hs]
    gcol = [gc[:, h:h + 1] for h in hs]
    grow = [gct[h:h + 1, :] for h in hs]
    decay = [jnp.where(tril, jnp.exp(jnp.where(tril, gcol[h] - grow[h], 0.0)), 0.0) for h in hs]
    kb = [k[h] * beta_all[:, h:h + 1] for h in hs]
    vb = [v[h] * beta_all[:, h:h + 1] for h in hs]
    kf = [k[h].astype(BF16) for h in hs]
    lower = [jnp.where(strict, lax.dot_general(kb[h].astype(BF16), kf[h], _NT, preferred_element_type=F32)
                       * decay[h], 0.0) for h in hs]
    attn = [(lax.dot_general(q[h].astype(BF16), kf[h], _NT, preferred_element_type=F32) * decay[h]).astype(BF16)
            for h in hs]
    pair = ((row ^ col) == 1) & strict
    tinv = [eye - jnp.where(pair, lower[h], 0.0) for h in hs]
    blk = 2
    while blk < c:
        m = ((row ^ col) < 2 * blk) & ((row & blk) != 0) & ((col & blk) == 0)
        tb = [tinv[h].astype(BF16) for h in hs]
        left = [jnp.dot(tb[h], jnp.where(m, lower[h], 0.0).astype(BF16), preferred_element_type=F32).astype(BF16)
                for h in hs]
        tinv = [tinv[h] - jnp.dot(left[h], tb[h], preferred_element_type=F32) for h in hs]
        blk *= 2
    eg = [jnp.exp(gcol[h]) for h in hs]
    uw = [jnp.dot(tinv[h].astype(BF16),
                  jnp.concatenate([vb[h], kb[h] * eg[h]], axis=-1).astype(BF16), preferred_element_type=F32)
          for h in hs]
    st = [st_ref[h] for h in hs]
    ws = [jnp.dot(jnp.concatenate([uw[h][:, hd:], q[h] * eg[h]], axis=0).astype(BF16), st[h].astype(BF16),
                  preferred_element_type=F32) for h in hs]
    vnb = [(uw[h][:, :hd] - ws[h][:c]).astype(BF16) for h in hs]
    o = [ws[h][c:] + jnp.dot(attn[h], vnb[h], preferred_element_type=F32) for h in hs]
    for h in hs:
        g_last = grow[h][:, c - 1:c]
        k_dec = (k[h] * jnp.exp(g_last - gcol[h])).astype(BF16)
        st_ref[h] = st[h] * jnp.exp(g_last) + lax.dot_general(k_dec, vnb[h], _TN, preferred_element_type=F32)
    o_ref[0] = jnp.concatenate(o, axis=-1).astype(BF16)


def _deltanet(qkv, bd, bdt, cw, alog, dtb):
    b, s, _ = qkv.shape
    c = DN_CHUNK
    alog2, dtb2 = alog.reshape(1, DN_HEADS), dtb.reshape(1, DN_HEADS)
    alogt, dtbt = alog.reshape(DN_HEADS, 1), dtb.reshape(DN_HEADS, 1)
    return pl.pallas_call(
        _dn_kernel,
        grid=(b, s // c),
        in_specs=[pl.BlockSpec((1, c, 3 * DN_WIDTH), lambda i, j: (i, j, 0)),
                  pl.BlockSpec((1, c, 2 * DN_HEADS), lambda i, j: (i, j, 0)),
                  pl.BlockSpec((1, 2 * DN_HEADS, c), lambda i, j: (i, 0, j)),
                  _const_spec(cw.shape), _const_spec(alog2.shape), _const_spec(dtb2.shape),
                  _const_spec(alogt.shape), _const_spec(dtbt.shape)],
        out_specs=pl.BlockSpec((1, c, DN_WIDTH), lambda i, j: (i, j, 0)),
        out_shape=jax.ShapeDtypeStruct((b, s, DN_WIDTH), BF16),
        scratch_shapes=[pltpu.VMEM((c + CONV_HALO, 3 * DN_WIDTH), F32),
                        pltpu.VMEM((DN_HEADS, DN_HEAD_DIM, DN_HEAD_DIM), F32)],
        compiler_params=_params("parallel", "arbitrary"),
        name="deltanet",
    )(qkv, bd, bdt, cw, alog2, dtb2, alogt, dtbt)


def _rms(x, w):
    return x * lax.rsqrt(jnp.mean(x * x, axis=-1, keepdims=True) + EPS) * w


def _post_kernel(o_ref, z_ref, ga_ref, gb_ref, ya_ref, x_ref, dnw_ref, wdn_ref, wmix_ref, fnw_ref,
                 h1_ref, xn2t_ref):
    o = o_ref[...].astype(F32)
    z = z_ref[...].astype(F32)
    parts = []
    for h in range(DN_HEADS):
        sl = slice(h * DN_HEAD_DIM, (h + 1) * DN_HEAD_DIM)
        parts.append(_rms(o[:, sl], dnw_ref[...]))
    on = jnp.concatenate(parts, axis=-1) * (z * _sigmoid(z))
    yb = jnp.dot(on.astype(BF16), wdn_ref[...], preferred_element_type=F32)
    merged = (_sigmoid(ga_ref[...].astype(F32)) * ya_ref[...].astype(F32)
              + _sigmoid(gb_ref[...].astype(F32)) * yb)
    h1 = x_ref[...] + jnp.dot(merged.astype(BF16), wmix_ref[...], preferred_element_type=F32)
    h1_ref[...] = h1
    xn2t_ref[...] = _rms(h1, fnw_ref[...]).T.astype(BF16)


def _post(o, z, ga, gb, ya, x2, dnw, wdn, wmix, fnw):
    n = x2.shape[0]
    t = POST_TILE
    row = pl.BlockSpec((t, D_MODEL), lambda i: (i, 0))
    return pl.pallas_call(
        _post_kernel,
        grid=(n // t,),
        in_specs=[row, row, row, row, row, row, _const_spec(dnw.shape), _const_spec(wdn.shape),
                  _const_spec(wmix.shape), _const_spec(fnw.shape)],
        out_specs=[row, pl.BlockSpec((D_MODEL, t), lambda i: (0, i))],
        out_shape=[jax.ShapeDtypeStruct((n, D_MODEL), F32), jax.ShapeDtypeStruct((D_MODEL, n), BF16)],
        compiler_params=_params("parallel"),
        name="post",
    )(o, z, ga, gb, ya, x2, dnw, wdn, wmix, fnw)


_MARK_BASE = -2.0 ** 120
_MARK_STEP = 1.0 / 32.0
_FAST_FLOOR = -2.0 ** 119


def _extract_fast(x, rounds, vals_ref=None):
    ranks, bad = [], []
    for l0 in range(0, x.shape[1], V7X_LANES):
        lanes = slice(l0, l0 + V7X_LANES)
        work = x[:, lanes]
        too_low = jnp.min(work, axis=0, keepdims=True) < _FAST_FLOOR
        for r in range(rounds):
            m = jnp.max(work, axis=0, keepdims=True)
            if vals_ref is not None:
                vals_ref[r:r + 1, lanes] = m
            work = jnp.where(work == m, _MARK_BASE * (1.0 + r * _MARK_STEP), work)
        hit = work <= _MARK_BASE
        ranks.append(jnp.where(hit, (work * (1.0 / _MARK_BASE) - 1.0) * (1.0 / _MARK_STEP), float(rounds)))
        miscount = jnp.abs(jnp.sum(hit.astype(F32), axis=0, keepdims=True) - float(rounds))
        bad.append(jnp.where(too_low, 1.0, miscount))
    ok = jnp.max(jnp.concatenate(bad, axis=1)) == 0.0
    return jnp.concatenate(ranks, axis=1), ok


def _extract_exact(work, rounds, vals_ref=None):
    rows = work.shape[0]
    iota = lax.broadcasted_iota(jnp.int32, work.shape, 0).astype(F32)
    rank = jnp.full(work.shape, float(rounds), F32)
    for r in range(rounds):
        m = jnp.max(work, axis=0, keepdims=True)
        idx = jnp.min(jnp.where(work == m, iota, float(rows)), axis=0, keepdims=True)
        hit = iota == idx
        rank = jnp.where(hit, float(r), rank)
        work = jnp.where(hit, -jnp.inf, work)
        if vals_ref is not None:
            vals_ref[r:r + 1, :] = m
    return rank


def _bcast_rows_bf16(row, rows):
    packed = jnp.broadcast_to(row, (BF16_SUBLANES, row.shape[1])).astype(BF16)
    return jnp.concatenate([packed] * (rows // BF16_SUBLANES), axis=0)


def _peer_kernel(xnt_ref, h1_ref, wq_ref, k1_ref, k2_ref, u_first_ref, u_odd_ref, u_even_ref,
                 vt_odd_ref, vt_even_ref, vt_last_ref, fw_ref, out_ref,
                 e1_ref, n1_ref, r2_ref, e2_ref, rk1_ref, rk2_ref, v1_ref, v2_ref, cand_ref, sel_ref, yt_ref,
                 h_even_ref, h_odd_ref, p_even_ref, p_odd_ref):
    ib = pl.program_id(1)
    nib = pl.num_programs(1)
    t = xnt_ref.shape[1]

    @pl.when(ib == 0)
    def _():
        yt_ref[...] = jnp.zeros(yt_ref.shape, F32)
        p_odd_ref[...] = jnp.zeros(p_odd_ref.shape, BF16)
        xnt = xnt_ref[...]
        h_even_ref[...] = jnp.dot(u_first_ref[...], xnt, preferred_element_type=F32)

        def head(h, carry):
            r0 = pl.multiple_of(h * 2 * PEER_HALF, 2 * PEER_HALF)
            q1 = jnp.dot(wq_ref[pl.ds(r0, PEER_HALF), :], xnt, preferred_element_type=F32)
            q2 = jnp.dot(wq_ref[pl.ds(r0 + PEER_HALF, PEER_HALF), :], xnt, preferred_element_type=F32)
            s1 = jnp.dot(k1_ref[h], q1.astype(BF16), preferred_element_type=F32)
            s2 = jnp.dot(k2_ref[h], q2.astype(BF16), preferred_element_type=F32)
            rank1, ok1 = _extract_fast(s1, PEER_TOPK, v1_ref)
            rank2, ok2 = _extract_fast(s2, PEER_TOPK, v2_ref)
            rk1_ref[...] = rank1
            rk2_ref[...] = rank2

            @pl.when(jnp.logical_not(ok1 & ok2))
            def _():
                rk1_ref[...] = _extract_exact(s1, PEER_TOPK, v1_ref)
                rk2_ref[...] = _extract_exact(s2, PEER_TOPK, v2_ref)

            cand_ref[...] = jnp.full(cand_ref.shape, -jnp.inf, F32)
            for n, (a, b) in enumerate(PEER_CANDS):
                cand_ref[n:n + 1, :] = v1_ref[a:a + 1, :] + v2_ref[b:b + 1, :]
            cand = cand_ref[...]
            valid = lax.broadcasted_iota(jnp.int32, cand.shape, 0) < len(PEER_CANDS)
            crank, ok3 = _extract_fast(jnp.where(valid, cand, _FAST_FLOOR), PEER_TOPK)
            sel_ref[...] = crank

            @pl.when(jnp.logical_not(ok3))
            def _():
                sel_ref[...] = _extract_exact(cand, PEER_TOPK)

            sel = sel_ref[...] < float(PEER_TOPK)
            cexp = jnp.where(sel, jnp.exp(jnp.where(sel, cand - cand[0:1, :], 0.0)), 0.0)
            inv_z = 1.0 / jnp.sum(cexp, axis=0, keepdims=True)
            self32 = sel.astype(F32)
            rank1 = rk1_ref[...]
            n1 = jnp.zeros(rank1.shape, F32)
            n = 0
            for a in range(PEER_TOPK):
                width = PEER_TOPK // (a + 1)
                n_a = jnp.sum(self32[n:n + width, :], axis=0, keepdims=True)
                n1 = jnp.where(rank1 == float(a), n_a, n1)
                n += width
            e1_ref[h] = jnp.exp(s1 - v1_ref[0:1, :]) * (0.5 * inv_z)
            n1_ref[h] = n1
            r2_ref[h] = rk2_ref[...].astype(BF16)
            e2_ref[h] = jnp.exp(s2 - v2_ref[0:1, :]).astype(BF16)
            return carry

        lax.fori_loop(0, PEER_HEADS, head, 0)

    th = t // 2

    def stage(blk, hf, u_next_ref, h_next_ref, h_cur_ref, p_cur_ref, vt_prev_ref, p_prev_ref):
        lanes = slice(hf * th, (hf + 1) * th)
        h_next_ref[:, lanes] = jnp.dot(u_next_ref[...], xnt_ref[:, lanes], preferred_element_type=F32)
        for kk in range(PEER_IBLK):
            i = blk * PEER_IBLK + kk
            rows = slice(kk * N_KEYS, (kk + 1) * N_KEYS)
            hk = h_cur_ref[rows, lanes]
            act = hk * (1.0 + lax.erf(hk * 0.7071067811865476))
            g = jnp.zeros((N_KEYS, th), BF16)
            for h in range(PEER_HEADS):
                e1row = _bcast_rows_bf16(e1_ref[h, pl.ds(i, 1), lanes], N_KEYS)
                n1row = _bcast_rows_bf16(n1_ref[h, pl.ds(i, 1), lanes], N_KEYS)
                g = g + e1row * jnp.where(r2_ref[h, :, lanes] < n1row, e2_ref[h, :, lanes], jnp.zeros((), BF16))
            p_cur_ref[rows, lanes] = g * act.astype(BF16)
        yt_ref[:, lanes] += jnp.dot(vt_prev_ref[...], p_prev_ref[:, lanes], preferred_element_type=F32)

    for hf in range(2):
        stage(2 * ib, hf, u_odd_ref, h_odd_ref, h_even_ref, p_even_ref, vt_odd_ref, p_odd_ref)
    for hf in range(2):
        stage(2 * ib + 1, hf, u_even_ref, h_even_ref, h_odd_ref, p_odd_ref, vt_even_ref, p_even_ref)

    @pl.when(ib == nib - 1)
    def _():
        yt = yt_ref[...] + jnp.dot(vt_last_ref[...], p_odd_ref[...], preferred_element_type=F32)
        hfin = h1_ref[...] + yt.T
        out_ref[...] = _rms(hfin, fw_ref[...])


def _peer(xn2t, h1, wqt, k1, k2, u, vt, fw):
    n = xn2t.shape[1]
    t = PEER_TILE
    eb = PEER_IBLK * N_KEYS
    nblk = (N_KEYS * N_KEYS) // eb
    assert nblk % 2 == 0
    tok = pl.BlockSpec((t, D_MODEL), lambda i, j: (i, 0))
    big = pltpu.VMEM((PEER_HEADS, N_KEYS, t), F32)
    bigb = pltpu.VMEM((PEER_HEADS, N_KEYS, t), BF16)
    keys = pltpu.VMEM((N_KEYS, t), F32)
    u_spec = lambda f: pl.BlockSpec((eb, D_MODEL), lambda i, j: (f(j), 0))
    vt_spec = lambda f: pl.BlockSpec((D_MODEL, eb), lambda i, j: (0, f(j)))
    return pl.pallas_call(
        _peer_kernel,
        grid=(n // t, nblk // 2),
        in_specs=[pl.BlockSpec((D_MODEL, t), lambda i, j: (0, i)), tok,
                  _const_spec(wqt.shape), _const_spec(k1.shape), _const_spec(k2.shape),
                  u_spec(lambda j: 0), u_spec(lambda j: 2 * j + 1),
                  u_spec(lambda j: jnp.minimum(2 * j + 2, nblk - 1)),
                  vt_spec(lambda j: jnp.maximum(2 * j - 1, 0)), vt_spec(lambda j: 2 * j),
                  vt_spec(lambda j: nblk - 1),
                  _const_spec(fw.shape)],
        out_specs=tok,
        out_shape=jax.ShapeDtypeStruct((n, D_MODEL), F32),
        scratch_shapes=[big, big, bigb, bigb, keys, keys,
                        pltpu.VMEM((PEER_TOPK, t), F32), pltpu.VMEM((PEER_TOPK, t), F32),
                        pltpu.VMEM((PEER_CAND_ROWS, t), F32), pltpu.VMEM((PEER_CAND_ROWS, t), F32),
                        pltpu.VMEM((D_MODEL, t), F32),
                        pltpu.VMEM((eb, t), F32), pltpu.VMEM((eb, t), F32),
                        pltpu.VMEM((eb, t), BF16), pltpu.VMEM((eb, t), BF16)],
        compiler_params=_params("parallel", "arbitrary"),
        name="peer",
    )(xn2t, h1, wqt, k1, k2, u, u, u, vt, vt, vt, fw)


def kernel(x, mix_norm_w, w_in, pool_w, pool_scale, conv_w, a_log, dt_bias, dn_norm_w, w_pool_up,
           w_dn_up, w_mix_out, ffn_norm_w, peer_w_query, peer_keys_1, peer_keys_2, peer_down, peer_up,
           final_norm_w):
    b, s, d = x.shape
    n = b * s
    assert d == D_MODEL and s % POOL_TILE == 0 and s % DN_CHUNK == 0
    assert n % IN_TILE == 0 and n % POST_TILE == 0 and n % PEER_TILE == 0
    assert w_in.shape[0] == 1, "single-layer block"
    l = 0
    h = x.reshape(n, d)

    c0 = POOL_WIDTH
    c1 = c0 + 3 * DN_WIDTH
    c2 = c1 + DN_WIDTH
    c3 = c2 + 2 * DN_HEADS
    c4 = c3 + D_MODEL
    wi = w_in[l]
    xa, qkv, z, ga, gb, bd = _in_proj(
        h, mix_norm_w[l].reshape(1, d),
        wi[:, :c0].astype(BF16), wi[:, c0:c1].astype(BF16), wi[:, c1:c2].astype(BF16),
        wi[:, c3:c4].astype(BF16), wi[:, c4:].astype(BF16), wi[:, c2:c3])

    ya = _pool(xa.reshape(b, s, POOL_WIDTH), pool_w[l].astype(BF16), pool_scale[l].reshape(1, POOL_WIDTH),
               w_pool_up[l].astype(BF16))

    bd3 = bd.reshape(b, s, 2 * DN_HEADS)
    o = _deltanet(qkv.reshape(b, s, 3 * DN_WIDTH), bd3, bd3.transpose(0, 2, 1), conv_w[l], a_log[l],
                  dt_bias[l])

    h1, xn2t = _post(o.reshape(n, DN_WIDTH), z, ga, gb, ya.reshape(n, d), h,
                    dn_norm_w[l].reshape(1, DN_HEAD_DIM), w_dn_up[l].astype(BF16),
                    w_mix_out[l].astype(BF16), ffn_norm_w[l].reshape(1, d))

    out = _peer(xn2t, h1, peer_w_query[l].T.astype(BF16), peer_keys_1[l].astype(BF16),
                peer_keys_2[l].astype(BF16), peer_down[l].astype(BF16), peer_up[l].T.astype(BF16),
                final_norm_w.reshape(1, d))
    return out.reshape(b, s, d)
```

```python
import functools

import jax
import jax.numpy as jnp
from jax import lax
from jax.experimental import pallas as pl
from jax.experimental.pallas import tpu as pltpu

F32 = jnp.float32
BF16 = jnp.bfloat16
EPS = 1e-6

D_MODEL = 1024
POOL_WINDOWS = (2, 4, 8, 16)
POOL_GROUP_DIM = 128
POOL_WIDTH = 512
POOL_HALO = 16
DN_HEADS = 8
DN_HEAD_DIM = 128
DN_WIDTH = DN_HEADS * DN_HEAD_DIM
CONV_WIDTH = 4
CONV_HALO = 8
DN_CHUNK = 128
PEER_HEADS = 8
N_KEYS = 128
PEER_TOPK = 16
PEER_HALF = 128

V7X_LANES = 128
BF16_SUBLANES = 16
VMEM_LIMIT = 56 * 1024 * 1024

IN_TILE = 512
POOL_TILE = 512
POST_TILE = 256
PEER_TILE = 512
PEER_IBLK = 4

PEER_CANDS = tuple((a, b) for a in range(PEER_TOPK) for b in range(PEER_TOPK)
                   if (a + 1) * (b + 1) <= PEER_TOPK)
PEER_CAND_ROWS = 56

_NT = (((1,), (1,)), ((), ()))
_TN = (((0,), (0,)), ((), ()))


def _params(*sem):
    return pltpu.CompilerParams(dimension_semantics=sem, vmem_limit_bytes=VMEM_LIMIT)


def _const_spec(shape):
    return pl.BlockSpec(shape, lambda *_: (0,) * len(shape))


def _resident_spec(shape):
    return pl.BlockSpec(shape, lambda *_: (0,) * len(shape), pipeline_mode=pl.Buffered(1))


def _sigmoid(x):
    return 1.0 / (1.0 + jnp.exp(-x))


def _in_proj_kernel(x_ref, nw_ref, wxa_ref, wqkv_ref, wz_ref, wga_ref, wgb_ref, wbd_ref,
                    xa_ref, qkv_ref, z_ref, ga_ref, gb_ref, bd_ref):
    x = x_ref[...]
    xn = x * lax.rsqrt(jnp.mean(x * x, axis=-1, keepdims=True) + EPS) * nw_ref[...]
    xb = xn.astype(BF16)
    xa_ref[...] = jnp.dot(xb, wxa_ref[...], preferred_element_type=F32)
    qkv_ref[...] = jnp.dot(xb, wqkv_ref[...], preferred_element_type=F32)
    z_ref[...] = jnp.dot(xb, wz_ref[...], preferred_element_type=F32).astype(BF16)
    ga_ref[...] = jnp.dot(xb, wga_ref[...], preferred_element_type=F32).astype(BF16)
    gb_ref[...] = jnp.dot(xb, wgb_ref[...], preferred_element_type=F32).astype(BF16)
    bd_ref[...] = jnp.dot(xn, wbd_ref[...], preferred_element_type=F32,
                          precision=lax.Precision.HIGHEST)


def _in_proj(x2, nw, wxa, wqkv, wz, wga, wgb, wbd):
    n = x2.shape[0]
    t = IN_TILE
    row = lambda w: pl.BlockSpec((t, w), lambda i: (i, 0))
    widths = (POOL_WIDTH, 3 * DN_WIDTH, DN_WIDTH, D_MODEL, D_MODEL, 2 * DN_HEADS)
    dtypes = (F32, F32, BF16, BF16, BF16, F32)
    return pl.pallas_call(
        _in_proj_kernel,
        grid=(n // t,),
        in_specs=[row(D_MODEL), _const_spec(nw.shape), _resident_spec(wxa.shape), _resident_spec(wqkv.shape),
                  _resident_spec(wz.shape), _resident_spec(wga.shape), _resident_spec(wgb.shape),
                  _const_spec(wbd.shape)],
        out_specs=[row(w) for w in widths],
        out_shape=[jax.ShapeDtypeStruct((n, w), dt) for w, dt in zip(widths, dtypes)],
        compiler_params=_params("parallel"),
        name="in_proj",
    )(x2, nw, wxa, wqkv, wz, wga, wgb, wbd)


def _pool_kernel(xa_ref, pw_ref, ps_ref, wup_ref, ya_ref, buf_ref):
    s = pl.program_id(1)
    t = xa_ref.shape[1]

    @pl.when(s == 0)
    def _():
        buf_ref[0:POOL_HALO, :] = jnp.zeros((POOL_HALO, POOL_WIDTH), F32)

    @pl.when(s > 0)
    def _():
        buf_ref[0:POOL_HALO, :] = buf_ref[t:t + POOL_HALO, :]

    buf_ref[POOL_HALO:, :] = xa_ref[0]
    pos = s * t + lax.broadcasted_iota(jnp.int32, (t, 1), 0)
    ys = []
    for g, win in enumerate(POOL_WINDOWS):
        sl = slice(g * POOL_GROUP_DIM, (g + 1) * POOL_GROUP_DIM)
        xg = buf_ref[POOL_HALO:POOL_HALO + t, sl]
        acc = xg
        for k in range(1, win):
            acc = acc + buf_ref[POOL_HALO - k:POOL_HALO - k + t, sl]
        cnt = jnp.minimum(pos + 1, win).astype(F32)
        pooled = acc / cnt - xg
        y = jnp.dot(pooled.astype(BF16), pw_ref[g], preferred_element_type=F32)
        ys.append(y * ps_ref[:, sl])
    y = jnp.concatenate(ys, axis=-1)
    ya_ref[0] = jnp.dot(y.astype(BF16), wup_ref[...], preferred_element_type=F32).astype(BF16)


def _pool(xa, pw, ps, wup):
    b, s, _ = xa.shape
    t = POOL_TILE
    return pl.pallas_call(
        _pool_kernel,
        grid=(b, s // t),
        in_specs=[pl.BlockSpec((1, t, POOL_WIDTH), lambda i, j: (i, j, 0)),
                  _const_spec(pw.shape), _const_spec(ps.shape), _const_spec(wup.shape)],
        out_specs=pl.BlockSpec((1, t, D_MODEL), lambda i, j: (i, j, 0)),
        out_shape=jax.ShapeDtypeStruct((b, s, D_MODEL), BF16),
        scratch_shapes=[pltpu.VMEM((t + POOL_HALO, POOL_WIDTH), F32)],
        compiler_params=_params("parallel", "arbitrary"),
        name="pool",
    )(xa, pw, ps, wup)


def _softplus(x):
    return jnp.maximum(x, 0.0) + jnp.log1p(jnp.exp(-jnp.abs(x)))


def _dn_kernel(qkv_ref, bd_ref, bdt_ref, cw_ref, alog_ref, dtb_ref, alogt_ref, dtbt_ref,
               o_ref, cbuf_ref, st_ref):
    s = pl.program_id(1)
    c = DN_CHUNK
    hd = DN_HEAD_DIM

    @pl.when(s == 0)
    def _():
        cbuf_ref[0:CONV_HALO, :] = jnp.zeros((CONV_HALO, 3 * DN_WIDTH), F32)
        st_ref[...] = jnp.zeros(st_ref.shape, F32)

    @pl.when(s > 0)
    def _():
        cbuf_ref[0:CONV_HALO, :] = cbuf_ref[c:c + CONV_HALO, :]

    cbuf_ref[CONV_HALO:, :] = qkv_ref[0]

    bd = bd_ref[0]
    bdt = bdt_ref[0]
    beta_all = _sigmoid(bd[:, 0:DN_HEADS])
    g_all = -jnp.exp(alog_ref[...]) * _softplus(bd[:, DN_HEADS:] + dtb_ref[...])
    g_allt = -jnp.exp(alogt_ref[...]) * _softplus(bdt[DN_HEADS:, :] + dtbt_ref[...])

    row = lax.broadcasted_iota(jnp.int32, (c, c), 0)
    col = lax.broadcasted_iota(jnp.int32, (c, c), 1)
    tril = row >= col
    strict = row > col
    gc = jnp.dot(tril.astype(F32), g_all, preferred_element_type=F32, precision=lax.Precision.HIGHEST)
    gct = jnp.dot(g_allt, (row <= col).astype(F32), preferred_element_type=F32,
                  precision=lax.Precision.HIGHEST)
    eye = (row == col).astype(F32)

    def conv_silu(lane0):
        sl = slice(lane0, lane0 + hd)
        y = cw_ref[0:1, sl] * cbuf_ref[CONV_HALO - 3:CONV_HALO - 3 + c, sl]
        for j in range(1, CONV_WIDTH):
            y = y + cw_ref[j:j + 1, sl] * cbuf_ref[CONV_HALO - 3 + j:CONV_HALO - 3 + j + c, sl]
        return y * _sigmoid(y)

    def l2n(v):
        return v * lax.rsqrt(jnp.sum(v * v, axis=-1, keepdims=True) + EPS)

    hs = range(DN_HEADS)
    q = [l2n(conv_silu(h * hd)) * (hd ** -0.5) for h in hs]
    k = [l2n(conv_silu(DN_WIDTH + h * hd)) for h in hs]
    v = [conv_silu(2 * DN_WIDTH + h * hd) for h in hs]
    gcol = [gc[:, h:h + 1] for h in hs]
    grow = [gct[h:h + 1, :] for h in hs]
    decay = [jnp.where(tril, jnp.exp(jnp.where(tril, gcol[h] - grow[h], 0.0)), 0.0) for h in hs]
    kb = [k[h] * beta_all[:, h:h + 1] for h in hs]
    vb = [v[h] * beta_all[:, h:h + 1] for h in hs]
    kf = [k[h].astype(BF16) for h in hs]
    lower = [jnp.where(strict, lax.dot_general(kb[h].astype(BF16), kf[h], _NT, preferred_element_type=F32)
                       * decay[h], 0.0) for h in hs]
    attn = [(lax.dot_general(q[h].astype(BF16), kf[h], _NT, preferred_element_type=F32) * decay[h]).astype(BF16)
            for h in hs]
    pair = ((row ^ col) == 1) & strict
    tinv = [eye - jnp.where(pair, lower[h], 0.0) for h in hs]
    blk = 2
    while blk < c:
        m = ((row ^ col) < 2 * blk) & ((row & blk) != 0) & ((col & blk) == 0)
        tb = [tinv[h].astype(BF16) for h in hs]
        left = [jnp.dot(tb[h], jnp.where(m, lower[h], 0.0).astype(BF16), preferred_element_type=F32).astype(BF16)
                for h in hs]
        tinv = [tinv[h] - jnp.dot(left[h], tb[h], preferred_element_type=F32) for h in hs]
        blk *= 2
    eg = [jnp.exp(gcol[h]) for h in hs]
    uw = [jnp.dot(tinv[h].astype(BF16),
                  jnp.concatenate([vb[h], kb[h] * eg[h]], axis=-1).astype(BF16), preferred_element_type=F32)
          for h in hs]
    st = [st_ref[h] for h in hs]
    ws = [jnp.dot(jnp.concatenate([uw[h][:, hd:], q[h] * eg[h]], axis=0).astype(BF16), st[h].astype(BF16),
                  preferred_element_type=F32) for h in hs]
    vnb = [(uw[h][:, :hd] - ws[h][:c]).astype(BF16) for h in hs]
    o = [ws[h][c:] + jnp.dot(attn[h], vnb[h], preferred_element_type=F32) for h in hs]
    for h in hs:
        g_last = grow[h][:, c - 1:c]
        k_dec = (k[h] * jnp.exp(g_last - gcol[h])).astype(BF16)
        st_ref[h] = st[h] * jnp.exp(g_last) + lax.dot_general(k_dec, vnb[h], _TN, preferred_element_type=F32)
    o_ref[0] = jnp.concatenate(o, axis=-1).astype(BF16)


def _deltanet(qkv, bd, bdt, cw, alog, dtb):
    b, s, _ = qkv.shape
    c = DN_CHUNK
    alog2, dtb2 = alog.reshape(1, DN_HEADS), dtb.reshape(1, DN_HEADS)
    alogt, dtbt = alog.reshape(DN_HEADS, 1), dtb.reshape(DN_HEADS, 1)
    return pl.pallas_call(
        _dn_kernel,
        grid=(b, s // c),
        in_specs=[pl.BlockSpec((1, c, 3 * DN_WIDTH), lambda i, j: (i, j, 0)),
                  pl.BlockSpec((1, c, 2 * DN_HEADS), lambda i, j: (i, j, 0)),
                  pl.BlockSpec((1, 2 * DN_HEADS, c), lambda i, j: (i, 0, j)),
                  _const_spec(cw.shape), _const_spec(alog2.shape), _const_spec(dtb2.shape),
                  _const_spec(alogt.shape), _const_spec(dtbt.shape)],
        out_specs=pl.BlockSpec((1, c, DN_WIDTH), lambda i, j: (i, j, 0)),
        out_shape=jax.ShapeDtypeStruct((b, s, DN_WIDTH), BF16),
        scratch_shapes=[pltpu.VMEM((c + CONV_HALO, 3 * DN_WIDTH), F32),
                        pltpu.VMEM((DN_HEADS, DN_HEAD_DIM, DN_HEAD_DIM), F32)],
        compiler_params=_params("parallel", "arbitrary"),
        name="deltanet",
    )(qkv, bd, bdt, cw, alog2, dtb2, alogt, dtbt)


def _rms(x, w):
    return x * lax.rsqrt(jnp.mean(x * x, axis=-1, keepdims=True) + EPS) * w


def _post_kernel(o_ref, z_ref, ga_ref, gb_ref, ya_ref, x_ref, dnw_ref, wdn_ref, wmix_ref, fnw_ref,
                 h1_ref, xn2t_ref):
    o = o_ref[...].astype(F32)
    z = z_ref[...].astype(F32)
    parts = []
    for h in range(DN_HEADS):
        sl = slice(h * DN_HEAD_DIM, (h + 1) * DN_HEAD_DIM)
        parts.append(_rms(o[:, sl], dnw_ref[...]))
    on = jnp.concatenate(parts, axis=-1) * (z * _sigmoid(z))
    yb = jnp.dot(on.astype(BF16), wdn_ref[...], preferred_element_type=F32)
    merged = (_sigmoid(ga_ref[...].astype(F32)) * ya_ref[...].astype(F32)
              + _sigmoid(gb_ref[...].astype(F32)) * yb)
    h1 = x_ref[...] + jnp.dot(merged.astype(BF16), wmix_ref[...], preferred_element_type=F32)
    h1_ref[...] = h1
    xn2t_ref[...] = _rms(h1, fnw_ref[...]).T.astype(BF16)


def _post(o, z, ga, gb, ya, x2, dnw, wdn, wmix, fnw):
    n = x2.shape[0]
    t = POST_TILE
    row = pl.BlockSpec((t, D_MODEL), lambda i: (i, 0))
    return pl.pallas_call(
        _post_kernel,
        grid=(n // t,),
        in_specs=[row, row, row, row, row, row, _const_spec(dnw.shape), _const_spec(wdn.shape),
                  _const_spec(wmix.shape), _const_spec(fnw.shape)],
        out_specs=[row, pl.BlockSpec((D_MODEL, t), lambda i: (0, i))],
        out_shape=[jax.ShapeDtypeStruct((n, D_MODEL), F32), jax.ShapeDtypeStruct((D_MODEL, n), BF16)],
        compiler_params=_params("parallel"),
        name="post",
    )(o, z, ga, gb, ya, x2, dnw, wdn, wmix, fnw)


_MARK_BASE = -2.0 ** 120
_MARK_STEP = 1.0 / 32.0
_FAST_FLOOR = -2.0 ** 119


def _extract_fast(x, rounds, vals_ref=None):
    ranks = []
    bad = jnp.zeros((1, V7X_LANES), F32)
    for l0 in range(0, x.shape[1], V7X_LANES):
        lanes = slice(l0, l0 + V7X_LANES)
        work = x[:, lanes]
        too_low = jnp.min(work, axis=0, keepdims=True) < _FAST_FLOOR
        for r in range(rounds):
            m = jnp.max(work, axis=0, keepdims=True)
            if vals_ref is not None:
                vals_ref[r:r + 1, lanes] = m
            work = jnp.where(work == m, _MARK_BASE * (1.0 + r * _MARK_STEP), work)
        hit = work <= _MARK_BASE
        ranks.append(jnp.where(hit, (work * (1.0 / _MARK_BASE) - 1.0) * (1.0 / _MARK_STEP), float(rounds)))
        miscount = jnp.abs(jnp.sum(hit.astype(F32), axis=0, keepdims=True) - float(rounds))
        bad = bad + jnp.where(too_low, 1.0, miscount)
    return jnp.concatenate(ranks, axis=1), bad


def _extract_exact(work, rounds, vals_ref=None):
    rows = work.shape[0]
    iota = lax.broadcasted_iota(jnp.int32, work.shape, 0).astype(F32)
    rank = jnp.full(work.shape, float(rounds), F32)
    for r in range(rounds):
        m = jnp.max(work, axis=0, keepdims=True)
        idx = jnp.min(jnp.where(work == m, iota, float(rows)), axis=0, keepdims=True)
        hit = iota == idx
        rank = jnp.where(hit, float(r), rank)
        work = jnp.where(hit, -jnp.inf, work)
        if vals_ref is not None:
            vals_ref[r:r + 1, :] = m
    return rank


def _bcast_rows_bf16(row, rows):
    packed = jnp.broadcast_to(row, (BF16_SUBLANES, row.shape[1])).astype(BF16)
    return jnp.concatenate([packed] * (rows // BF16_SUBLANES), axis=0)


def _peer_kernel(xnt_ref, h1_ref, wq_ref, k1_ref, k2_ref, u_first_ref, u_odd_ref, u_even_ref,
                 vt_odd_ref, vt_even_ref, vt_last_ref, fw_ref, out_ref,
                 e1_ref, n1_ref, r2_ref, e2_ref, q_ref, v1_ref, v2_ref, cand_ref, yt_ref,
                 h_even_ref, h_odd_ref, p_even_ref, p_odd_ref):
    ib = pl.program_id(1)
    nib = pl.num_programs(1)
    t = xnt_ref.shape[1]

    def head_tables(exact, h, bad):
        r0 = pl.multiple_of(h * 2 * PEER_HALF, 2 * PEER_HALF)
        s1 = jnp.dot(k1_ref[h], q_ref[pl.ds(r0, PEER_HALF), :].astype(BF16),
                     preferred_element_type=F32)
        s2 = jnp.dot(k2_ref[h], q_ref[pl.ds(r0 + PEER_HALF, PEER_HALF), :].astype(BF16),
                     preferred_element_type=F32)
        if exact:
            rank1 = _extract_exact(s1, PEER_TOPK, v1_ref)
            rank2 = _extract_exact(s2, PEER_TOPK, v2_ref)
        else:
            rank1, bad1 = _extract_fast(s1, PEER_TOPK, v1_ref)
            rank2, bad2 = _extract_fast(s2, PEER_TOPK, v2_ref)
            bad = bad + bad1 + bad2
        cand_ref[...] = jnp.full(cand_ref.shape, -jnp.inf, F32)
        for n, (a, b) in enumerate(PEER_CANDS):
            cand_ref[n:n + 1, :] = v1_ref[a:a + 1, :] + v2_ref[b:b + 1, :]
        cand = cand_ref[...]
        if exact:
            crank = _extract_exact(cand, PEER_TOPK)
        else:
            valid = lax.broadcasted_iota(jnp.int32, cand.shape, 0) < len(PEER_CANDS)
            crank, bad3 = _extract_fast(jnp.where(valid, cand, _FAST_FLOOR), PEER_TOPK)
            bad = bad + bad3
        sel = crank < float(PEER_TOPK)
        cexp = jnp.where(sel, jnp.exp(jnp.where(sel, cand - cand[0:1, :], 0.0)), 0.0)
        inv_z = 1.0 / jnp.sum(cexp, axis=0, keepdims=True)
        self32 = sel.astype(F32)
        n1 = jnp.zeros(rank1.shape, F32)
        n = 0
        for a in range(PEER_TOPK):
            width = PEER_TOPK // (a + 1)
            n_a = jnp.sum(self32[n:n + width, :], axis=0, keepdims=True)
            n1 = jnp.where(rank1 == float(a), n_a, n1)
            n += width
        e1 = jnp.exp(s1 - v1_ref[0:1, :]) * (0.5 * inv_z)
        e1_ref[h] = e1
        n1_ref[h] = n1
        r2_ref[h] = rank2.astype(BF16)
        e2_ref[h] = jnp.exp(s2 - v2_ref[0:1, :]).astype(BF16)
        return bad

    @pl.when(ib == 0)
    def _():
        yt_ref[...] = jnp.zeros(yt_ref.shape, F32)
        p_odd_ref[...] = jnp.zeros(p_odd_ref.shape, BF16)
        xnt = xnt_ref[...]
        h_even_ref[...] = jnp.dot(u_first_ref[...], xnt, preferred_element_type=F32)
        q_ref[...] = jnp.dot(wq_ref[...], xnt, preferred_element_type=F32)
        no_bad = jnp.zeros((1, V7X_LANES), F32)
        bad = lax.fori_loop(0, PEER_HEADS, functools.partial(head_tables, False), no_bad)

        @pl.when(jnp.max(bad) != 0.0)
        def _():
            lax.fori_loop(0, PEER_HEADS, functools.partial(head_tables, True), no_bad)

    th = t // 2

    def stage(blk, hf, u_next_ref, h_next_ref, h_cur_ref, p_cur_ref, vt_prev_ref, p_prev_ref):
        lanes = slice(hf * th, (hf + 1) * th)
        h_next_ref[:, lanes] = jnp.dot(u_next_ref[...], xnt_ref[:, lanes], preferred_element_type=F32)
        for kk in range(PEER_IBLK):
            i = blk * PEER_IBLK + kk
            rows = slice(kk * N_KEYS, (kk + 1) * N_KEYS)
            hk = h_cur_ref[rows, lanes]
            act = hk * (1.0 + lax.erf(hk * 0.7071067811865476))
            g = jnp.zeros((N_KEYS, th), BF16)
            for h in range(PEER_HEADS):
                e1row = _bcast_rows_bf16(e1_ref[h, pl.ds(i, 1), lanes], N_KEYS)
                n1row = _bcast_rows_bf16(n1_ref[h, pl.ds(i, 1), lanes], N_KEYS)
                g = g + e1row * jnp.where(r2_ref[h, :, lanes] < n1row, e2_ref[h, :, lanes], jnp.zeros((), BF16))
            p_cur_ref[rows, lanes] = g * act.astype(BF16)
        yt_ref[:, lanes] += jnp.dot(vt_prev_ref[...], p_prev_ref[:, lanes], preferred_element_type=F32)

    for hf in range(2):
        stage(2 * ib, hf, u_odd_ref, h_odd_ref, h_even_ref, p_even_ref, vt_odd_ref, p_odd_ref)
    for hf in range(2):
        stage(2 * ib + 1, hf, u_even_ref, h_even_ref, h_odd_ref, p_odd_ref, vt_even_ref, p_even_ref)

    @pl.when(ib == nib - 1)
    def _():
        yt = yt_ref[...] + jnp.dot(vt_last_ref[...], p_odd_ref[...], preferred_element_type=F32)
        hfin = h1_ref[...] + yt.T
        out_ref[...] = _rms(hfin, fw_ref[...])


def _peer(xn2t, h1, wqt, k1, k2, u, vt, fw):
    n = xn2t.shape[1]
    t = PEER_TILE
    eb = PEER_IBLK * N_KEYS
    nblk = (N_KEYS * N_KEYS) // eb
    assert nblk % 2 == 0
    tok = pl.BlockSpec((t, D_MODEL), lambda i, j: (i, 0))
    big = pltpu.VMEM((PEER_HEADS, N_KEYS, t), F32)
    bigb = pltpu.VMEM((PEER_HEADS, N_KEYS, t), BF16)
    u_spec = lambda f: pl.BlockSpec((eb, D_MODEL), lambda i, j: (f(j), 0))
    vt_spec = lambda f: pl.BlockSpec((D_MODEL, eb), lambda i, j: (0, f(j)))
    return pl.pallas_call(
        _peer_kernel,
        grid=(n // t, nblk // 2),
        in_specs=[pl.BlockSpec((D_MODEL, t), lambda i, j: (0, i)), tok,
                  _const_spec(wqt.shape), _const_spec(k1.shape), _const_spec(k2.shape),
                  u_spec(lambda j: 0), u_spec(lambda j: 2 * j + 1),
                  u_spec(lambda j: jnp.minimum(2 * j + 2, nblk - 1)),
                  vt_spec(lambda j: jnp.maximum(2 * j - 1, 0)), vt_spec(lambda j: 2 * j),
                  vt_spec(lambda j: nblk - 1),
                  _const_spec(fw.shape)],
        out_specs=tok,
        out_shape=jax.ShapeDtypeStruct((n, D_MODEL), F32),
        scratch_shapes=[big, big, bigb, bigb,
                        pltpu.VMEM((2 * PEER_HALF * PEER_HEADS, t), F32),
                        pltpu.VMEM((PEER_TOPK, t), F32), pltpu.VMEM((PEER_TOPK, t), F32),
                        pltpu.VMEM((PEER_CAND_ROWS, t), F32),
                        pltpu.VMEM((D_MODEL, t), F32),
                        pltpu.VMEM((eb, t), F32), pltpu.VMEM((eb, t), F32),
                        pltpu.VMEM((eb, t), BF16), pltpu.VMEM((eb, t), BF16)],
        compiler_params=_params("parallel", "arbitrary"),
        name="peer",
    )(xn2t, h1, wqt, k1, k2, u, u, u, vt, vt, vt, fw)


def kernel(x, mix_norm_w, w_in, pool_w, pool_scale, conv_w, a_log, dt_bias, dn_norm_w, w_pool_up,
           w_dn_up, w_mix_out, ffn_norm_w, peer_w_query, peer_keys_1, peer_keys_2, peer_down, peer_up,
           final_norm_w):
    b, s, d = x.shape
    n = b * s
    assert d == D_MODEL and s % POOL_TILE == 0 and s % DN_CHUNK == 0
    assert n % IN_TILE == 0 and n % POST_TILE == 0 and n % PEER_TILE == 0
    assert w_in.shape[0] == 1, "single-layer block"
    l = 0
    h = x.reshape(n, d)

    c0 = POOL_WIDTH
    c1 = c0 + 3 * DN_WIDTH
    c2 = c1 + DN_WIDTH
    c3 = c2 + 2 * DN_HEADS
    c4 = c3 + D_MODEL
    wi = w_in[l]
    xa, qkv, z, ga, gb, bd = _in_proj(
        h, mix_norm_w[l].reshape(1, d),
        wi[:, :c0].astype(BF16), wi[:, c0:c1].astype(BF16), wi[:, c1:c2].astype(BF16),
        wi[:, c3:c4].astype(BF16), wi[:, c4:].astype(BF16), wi[:, c2:c3])

    ya = _pool(xa.reshape(b, s, POOL_WIDTH), pool_w[l].astype(BF16), pool_scale[l].reshape(1, POOL_WIDTH),
               w_pool_up[l].astype(BF16))

    bd3 = bd.reshape(b, s, 2 * DN_HEADS)
    o = _deltanet(qkv.reshape(b, s, 3 * DN_WIDTH), bd3, bd3.transpose(0, 2, 1), conv_w[l], a_log[l],
                  dt_bias[l])

    h1, xn2t = _post(o.reshape(n, DN_WIDTH), z, ga, gb, ya.reshape(n, d), h,
                    dn_norm_w[l].reshape(1, DN_HEAD_DIM), w_dn_up[l].astype(BF16),
                    w_mix_out[l].astype(BF16), ffn_norm_w[l].reshape(1, d))

    out = _peer(xn2t, h1, peer_w_query[l].T.astype(BF16), peer_keys_1[l].astype(BF16),
                peer_keys_2[l].astype(BF16), peer_down[l].astype(BF16), peer_up[l].T.astype(BF16),
                final_norm_w.reshape(1, d))
    return out.reshape(b, s, d)
```

```python
import functools

import jax
import jax.numpy as jnp
from jax import lax
from jax.experimental import pallas as pl
from jax.experimental.pallas import tpu as pltpu

F32 = jnp.float32
BF16 = jnp.bfloat16
EPS = 1e-6

D_MODEL = 1024
POOL_WINDOWS = (2, 4, 8, 16)
POOL_GROUP_DIM = 128
POOL_WIDTH = 512
POOL_HALO = 16
DN_HEADS = 8
DN_HEAD_DIM = 128
DN_WIDTH = DN_HEADS * DN_HEAD_DIM
CONV_WIDTH = 4
CONV_HALO = 8
DN_CHUNK = 128
PEER_HEADS = 8
N_KEYS = 128
PEER_TOPK = 16
PEER_HALF = 128

V7X_LANES = 128
BF16_SUBLANES = 16
VMEM_LIMIT = 56 * 1024 * 1024

IN_TILE = 512
POOL_TILE = 512
POST_TILE = 256
PEER_TILE = 512
PEER_IBLK = 4

PEER_CANDS = tuple((a, b) for a in range(PEER_TOPK) for b in range(PEER_TOPK)
                   if (a + 1) * (b + 1) <= PEER_TOPK)
PEER_CAND_ROWS = 56

_NT = (((1,), (1,)), ((), ()))
_TN = (((0,), (0,)), ((), ()))


def _params(*sem):
    return pltpu.CompilerParams(dimension_semantics=sem, vmem_limit_bytes=VMEM_LIMIT)


def _const_spec(shape):
    return pl.BlockSpec(shape, lambda *_: (0,) * len(shape))


def _resident_spec(shape):
    return pl.BlockSpec(shape, lambda *_: (0,) * len(shape), pipeline_mode=pl.Buffered(1))


def _sigmoid(x):
    return 1.0 / (1.0 + jnp.exp(-x))


def _in_proj_kernel(x_ref, nw_ref, wxa_ref, wqkv_ref, wz_ref, wga_ref, wgb_ref, wbd_ref,
                    xa_ref, qkv_ref, z_ref, ga_ref, gb_ref, bd_ref):
    x = x_ref[...]
    xn = x * lax.rsqrt(jnp.mean(x * x, axis=-1, keepdims=True) + EPS) * nw_ref[...]
    xb = xn.astype(BF16)
    xa_ref[...] = jnp.dot(xb, wxa_ref[...], preferred_element_type=F32)
    qkv_ref[...] = jnp.dot(xb, wqkv_ref[...], preferred_element_type=F32)
    z_ref[...] = jnp.dot(xb, wz_ref[...], preferred_element_type=F32).astype(BF16)
    ga_ref[...] = jnp.dot(xb, wga_ref[...], preferred_element_type=F32).astype(BF16)
    gb_ref[...] = jnp.dot(xb, wgb_ref[...], preferred_element_type=F32).astype(BF16)
    bd_ref[...] = jnp.dot(xn, wbd_ref[...], preferred_element_type=F32,
                          precision=lax.Precision.HIGHEST)


def _in_proj(x2, nw, wxa, wqkv, wz, wga, wgb, wbd):
    n = x2.shape[0]
    t = IN_TILE
    row = lambda w: pl.BlockSpec((t, w), lambda i: (i, 0))
    widths = (POOL_WIDTH, 3 * DN_WIDTH, DN_WIDTH, D_MODEL, D_MODEL, 2 * DN_HEADS)
    dtypes = (F32, F32, BF16, BF16, BF16, F32)
    return pl.pallas_call(
        _in_proj_kernel,
        grid=(n // t,),
        in_specs=[row(D_MODEL), _const_spec(nw.shape), _resident_spec(wxa.shape), _resident_spec(wqkv.shape),
                  _resident_spec(wz.shape), _resident_spec(wga.shape), _resident_spec(wgb.shape),
                  _const_spec(wbd.shape)],
        out_specs=[row(w) for w in widths],
        out_shape=[jax.ShapeDtypeStruct((n, w), dt) for w, dt in zip(widths, dtypes)],
        compiler_params=_params("parallel"),
        name="in_proj",
    )(x2, nw, wxa, wqkv, wz, wga, wgb, wbd)


def _pool_kernel(xa_ref, pw_ref, ps_ref, wup_ref, ya_ref, buf_ref):
    s = pl.program_id(1)
    t = xa_ref.shape[1]

    @pl.when(s == 0)
    def _():
        buf_ref[0:POOL_HALO, :] = jnp.zeros((POOL_HALO, POOL_WIDTH), F32)

    @pl.when(s > 0)
    def _():
        buf_ref[0:POOL_HALO, :] = buf_ref[t:t + POOL_HALO, :]

    buf_ref[POOL_HALO:, :] = xa_ref[0]
    pos = s * t + lax.broadcasted_iota(jnp.int32, (t, 1), 0)
    ys = []
    for g, win in enumerate(POOL_WINDOWS):
        sl = slice(g * POOL_GROUP_DIM, (g + 1) * POOL_GROUP_DIM)
        xg = buf_ref[POOL_HALO:POOL_HALO + t, sl]
        acc = xg
        for k in range(1, win):
            acc = acc + buf_ref[POOL_HALO - k:POOL_HALO - k + t, sl]
        cnt = jnp.minimum(pos + 1, win).astype(F32)
        pooled = acc / cnt - xg
        y = jnp.dot(pooled.astype(BF16), pw_ref[g], preferred_element_type=F32)
        ys.append(y * ps_ref[:, sl])
    y = jnp.concatenate(ys, axis=-1)
    ya_ref[0] = jnp.dot(y.astype(BF16), wup_ref[...], preferred_element_type=F32).astype(BF16)


def _pool(xa, pw, ps, wup):
    b, s, _ = xa.shape
    t = POOL_TILE
    return pl.pallas_call(
        _pool_kernel,
        grid=(b, s // t),
        in_specs=[pl.BlockSpec((1, t, POOL_WIDTH), lambda i, j: (i, j, 0)),
                  _const_spec(pw.shape), _const_spec(ps.shape), _const_spec(wup.shape)],
        out_specs=pl.BlockSpec((1, t, D_MODEL), lambda i, j: (i, j, 0)),
        out_shape=jax.ShapeDtypeStruct((b, s, D_MODEL), BF16),
        scratch_shapes=[pltpu.VMEM((t + POOL_HALO, POOL_WIDTH), F32)],
        compiler_params=_params("parallel", "arbitrary"),
        name="pool",
    )(xa, pw, ps, wup)


def _softplus(x):
    return jnp.maximum(x, 0.0) + jnp.log1p(jnp.exp(-jnp.abs(x)))


def _dn_kernel(qkv_ref, bd_ref, bdt_ref, cw_ref, alog_ref, dtb_ref, alogt_ref, dtbt_ref,
               o_ref, cbuf_ref, st_ref):
    s = pl.program_id(1)
    c = DN_CHUNK
    hd = DN_HEAD_DIM

    @pl.when(s == 0)
    def _():
        cbuf_ref[0:CONV_HALO, :] = jnp.zeros((CONV_HALO, 3 * DN_WIDTH), F32)
        st_ref[...] = jnp.zeros(st_ref.shape, F32)

    @pl.when(s > 0)
    def _():
        cbuf_ref[0:CONV_HALO, :] = cbuf_ref[c:c + CONV_HALO, :]

    cbuf_ref[CONV_HALO:, :] = qkv_ref[0]

    bd = bd_ref[0]
    bdt = bdt_ref[0]
    beta_all = _sigmoid(bd[:, 0:DN_HEADS])
    g_all = -jnp.exp(alog_ref[...]) * _softplus(bd[:, DN_HEADS:] + dtb_ref[...])
    g_allt = -jnp.exp(alogt_ref[...]) * _softplus(bdt[DN_HEADS:, :] + dtbt_ref[...])

    row = lax.broadcasted_iota(jnp.int32, (c, c), 0)
    col = lax.broadcasted_iota(jnp.int32, (c, c), 1)
    tril = row >= col
    strict = row > col
    gc = jnp.dot(tril.astype(F32), g_all, preferred_element_type=F32, precision=lax.Precision.HIGHEST)
    gct = jnp.dot(g_allt, (row <= col).astype(F32), preferred_element_type=F32,
                  precision=lax.Precision.HIGHEST)
    eye = (row == col).astype(F32)

    def conv_silu(lane0):
        sl = slice(lane0, lane0 + hd)
        y = cw_ref[0:1, sl] * cbuf_ref[CONV_HALO - 3:CONV_HALO - 3 + c, sl]
        for j in range(1, CONV_WIDTH):
            y = y + cw_ref[j:j + 1, sl] * cbuf_ref[CONV_HALO - 3 + j:CONV_HALO - 3 + j + c, sl]
        return y * _sigmoid(y)

    def l2n(v):
        return v * lax.rsqrt(jnp.sum(v * v, axis=-1, keepdims=True) + EPS)

    hs = range(DN_HEADS)
    q = [l2n(conv_silu(h * hd)) * (hd ** -0.5) for h in hs]
    k = [l2n(conv_silu(DN_WIDTH + h * hd)) for h in hs]
    v = [conv_silu(2 * DN_WIDTH + h * hd) for h in hs]
    gcol = [gc[:, h:h + 1] for h in hs]
    grow = [gct[h:h + 1, :] for h in hs]
    decay = [jnp.where(tril, jnp.exp(jnp.where(tril, gcol[h] - grow[h], 0.0)), 0.0) for h in hs]
    kb = [k[h] * beta_all[:, h:h + 1] for h in hs]
    vb = [v[h] * beta_all[:, h:h + 1] for h in hs]
    kf = [k[h].astype(BF16) for h in hs]
    lower = [jnp.where(strict, lax.dot_general(kb[h].astype(BF16), kf[h], _NT, preferred_element_type=F32)
                       * decay[h], 0.0) for h in hs]
    attn = [(lax.dot_general(q[h].astype(BF16), kf[h], _NT, preferred_element_type=F32) * decay[h]).astype(BF16)
            for h in hs]
    pair = ((row ^ col) == 1) & strict
    tinv = [eye - jnp.where(pair, lower[h], 0.0) for h in hs]
    blk = 2
    while blk < c:
        m = ((row ^ col) < 2 * blk) & ((row & blk) != 0) & ((col & blk) == 0)
        tb = [tinv[h].astype(BF16) for h in hs]
        left = [jnp.dot(tb[h], jnp.where(m, lower[h], 0.0).astype(BF16), preferred_element_type=F32).astype(BF16)
                for h in hs]
        tinv = [tinv[h] - jnp.dot(left[h], tb[h], preferred_element_type=F32) for h in hs]
        blk *= 2
    eg = [jnp.exp(gcol[h]) for h in hs]
    uw = [jnp.dot(tinv[h].astype(BF16),
                  jnp.concatenate([vb[h], kb[h] * eg[h]], axis=-1).astype(BF16), preferred_element_type=F32)
          for h in hs]
    st = [st_ref[h] for h in hs]
    ws = [jnp.dot(jnp.concatenate([uw[h][:, hd:], q[h] * eg[h]], axis=0).astype(BF16), st[h].astype(BF16),
                  preferred_element_type=F32) for h in hs]
    vnb = [(uw[h][:, :hd] - ws[h][:c]).astype(BF16) for h in hs]
    o = [ws[h][c:] + jnp.dot(attn[h], vnb[h], preferred_element_type=F32) for h in hs]
    for h in hs:
        g_last = grow[h][:, c - 1:c]
        k_dec = (k[h] * jnp.exp(g_last - gcol[h])).astype(BF16)
        st_ref[h] = st[h] * jnp.exp(g_last) + lax.dot_general(k_dec, vnb[h], _TN, preferred_element_type=F32)
    o_ref[0] = jnp.concatenate(o, axis=-1).astype(BF16)


def _deltanet(qkv, bd, bdt, cw, alog, dtb):
    b, s, _ = qkv.shape
    c = DN_CHUNK
    alog2, dtb2 = alog.reshape(1, DN_HEADS), dtb.reshape(1, DN_HEADS)
    alogt, dtbt = alog.reshape(DN_HEADS, 1), dtb.reshape(DN_HEADS, 1)
    return pl.pallas_call(
        _dn_kernel,
        grid=(b, s // c),
        in_specs=[pl.BlockSpec((1, c, 3 * DN_WIDTH), lambda i, j: (i, j, 0)),
                  pl.BlockSpec((1, c, 2 * DN_HEADS), lambda i, j: (i, j, 0)),
                  pl.BlockSpec((1, 2 * DN_HEADS, c), lambda i, j: (i, 0, j)),
                  _const_spec(cw.shape), _const_spec(alog2.shape), _const_spec(dtb2.shape),
                  _const_spec(alogt.shape), _const_spec(dtbt.shape)],
        out_specs=pl.BlockSpec((1, c, DN_WIDTH), lambda i, j: (i, j, 0)),
        out_shape=jax.ShapeDtypeStruct((b, s, DN_WIDTH), BF16),
        scratch_shapes=[pltpu.VMEM((c + CONV_HALO, 3 * DN_WIDTH), F32),
                        pltpu.VMEM((DN_HEADS, DN_HEAD_DIM, DN_HEAD_DIM), F32)],
        compiler_params=_params("parallel", "arbitrary"),
        name="deltanet",
    )(qkv, bd, bdt, cw, alog2, dtb2, alogt, dtbt)


def _rms(x, w):
    return x * lax.rsqrt(jnp.mean(x * x, axis=-1, keepdims=True) + EPS) * w


def _post_kernel(o_ref, z_ref, ga_ref, gb_ref, ya_ref, x_ref, dnw_ref, wdn_ref, wmix_ref, fnw_ref,
                 h1_ref, xn2t_ref):
    o = o_ref[...].astype(F32)
    z = z_ref[...].astype(F32)
    parts = []
    for h in range(DN_HEADS):
        sl = slice(h * DN_HEAD_DIM, (h + 1) * DN_HEAD_DIM)
        parts.append(_rms(o[:, sl], dnw_ref[...]))
    on = jnp.concatenate(parts, axis=-1) * (z * _sigmoid(z))
    yb = jnp.dot(on.astype(BF16), wdn_ref[...], preferred_element_type=F32)
    merged = (_sigmoid(ga_ref[...].astype(F32)) * ya_ref[...].astype(F32)
              + _sigmoid(gb_ref[...].astype(F32)) * yb)
    h1 = x_ref[...] + jnp.dot(merged.astype(BF16), wmix_ref[...], preferred_element_type=F32)
    h1_ref[...] = h1
    xn2t_ref[...] = _rms(h1, fnw_ref[...]).T.astype(BF16)


def _post(o, z, ga, gb, ya, x2, dnw, wdn, wmix, fnw):
    n = x2.shape[0]
    t = POST_TILE
    row = pl.BlockSpec((t, D_MODEL), lambda i: (i, 0))
    return pl.pallas_call(
        _post_kernel,
        grid=(n // t,),
        in_specs=[row, row, row, row, row, row, _const_spec(dnw.shape), _const_spec(wdn.shape),
                  _const_spec(wmix.shape), _const_spec(fnw.shape)],
        out_specs=[row, pl.BlockSpec((D_MODEL, t), lambda i: (0, i))],
        out_shape=[jax.ShapeDtypeStruct((n, D_MODEL), F32), jax.ShapeDtypeStruct((D_MODEL, n), BF16)],
        compiler_params=_params("parallel"),
        name="post",
    )(o, z, ga, gb, ya, x2, dnw, wdn, wmix, fnw)


_MARK_BASE = -2.0 ** 120
_MARK_STEP = 1.0 / 32.0
_FAST_FLOOR = -2.0 ** 119


def _extract_fast(x, rounds, vals_ref=None):
    ranks, bad = [], []
    for l0 in range(0, x.shape[1], V7X_LANES):
        lanes = slice(l0, l0 + V7X_LANES)
        work = x[:, lanes]
        too_low = jnp.min(work, axis=0, keepdims=True) < _FAST_FLOOR
        for r in range(rounds):
            m = jnp.max(work, axis=0, keepdims=True)
            if vals_ref is not None:
                vals_ref[r:r + 1, lanes] = m
            work = jnp.where(work == m, _MARK_BASE * (1.0 + r * _MARK_STEP), work)
        hit = work <= _MARK_BASE
        ranks.append(jnp.where(hit, (work * (1.0 / _MARK_BASE) - 1.0) * (1.0 / _MARK_STEP), float(rounds)))
        miscount = jnp.abs(jnp.sum(hit.astype(F32), axis=0, keepdims=True) - float(rounds))
        bad.append(jnp.where(too_low, 1.0, miscount))
    ok = jnp.max(jnp.concatenate(bad, axis=1)) == 0.0
    return jnp.concatenate(ranks, axis=1), ok


def _extract_exact(work, rounds, vals_ref=None):
    rows = work.shape[0]
    iota = lax.broadcasted_iota(jnp.int32, work.shape, 0).astype(F32)
    rank = jnp.full(work.shape, float(rounds), F32)
    for r in range(rounds):
        m = jnp.max(work, axis=0, keepdims=True)
        idx = jnp.min(jnp.where(work == m, iota, float(rows)), axis=0, keepdims=True)
        hit = iota == idx
        rank = jnp.where(hit, float(r), rank)
        work = jnp.where(hit, -jnp.inf, work)
        if vals_ref is not None:
            vals_ref[r:r + 1, :] = m
    return rank


def _bcast_rows_bf16(row, rows):
    packed = jnp.broadcast_to(row, (BF16_SUBLANES, row.shape[1])).astype(BF16)
    return jnp.concatenate([packed] * (rows // BF16_SUBLANES), axis=0)


def _peer_kernel(xnt_ref, h1_ref, wq_ref, k1_ref, k2_ref, u_first_ref, u_odd_ref, u_even_ref,
                 vt_odd_ref, vt_even_ref, vt_last_ref, fw_ref, out_ref,
                 e1_ref, n1_ref, r2_ref, e2_ref, rk1_ref, rk2_ref, v1_ref, v2_ref, cand_ref, sel_ref, yt_ref,
                 h_even_ref, h_odd_ref, p_even_ref, p_odd_ref, q_ref):
    ib = pl.program_id(1)
    nib = pl.num_programs(1)
    t = xnt_ref.shape[1]

    @pl.when(ib == 0)
    def _():
        yt_ref[...] = jnp.zeros(yt_ref.shape, F32)
        p_odd_ref[...] = jnp.zeros(p_odd_ref.shape, BF16)
        xnt = xnt_ref[...]
        h_even_ref[...] = jnp.dot(u_first_ref[...], xnt, preferred_element_type=F32)
        q_ref[...] = jnp.dot(wq_ref[...], xnt, preferred_element_type=F32).astype(BF16)

        def head(h, carry):
            r0 = pl.multiple_of(h * 2 * PEER_HALF, 2 * PEER_HALF)
            s1 = jnp.dot(k1_ref[h], q_ref[pl.ds(r0, PEER_HALF), :], preferred_element_type=F32)
            s2 = jnp.dot(k2_ref[h], q_ref[pl.ds(r0 + PEER_HALF, PEER_HALF), :], preferred_element_type=F32)
            rank1, ok1 = _extract_fast(s1, PEER_TOPK, v1_ref)
            rank2, ok2 = _extract_fast(s2, PEER_TOPK, v2_ref)
            rk1_ref[...] = rank1
            rk2_ref[...] = rank2

            @pl.when(jnp.logical_not(ok1 & ok2))
            def _():
                rk1_ref[...] = _extract_exact(s1, PEER_TOPK, v1_ref)
                rk2_ref[...] = _extract_exact(s2, PEER_TOPK, v2_ref)

            cand_ref[...] = jnp.full(cand_ref.shape, -jnp.inf, F32)
            for n, (a, b) in enumerate(PEER_CANDS):
                cand_ref[n:n + 1, :] = v1_ref[a:a + 1, :] + v2_ref[b:b + 1, :]
            cand = cand_ref[...]
            valid = lax.broadcasted_iota(jnp.int32, cand.shape, 0) < len(PEER_CANDS)
            crank, ok3 = _extract_fast(jnp.where(valid, cand, _FAST_FLOOR), PEER_TOPK)
            sel_ref[...] = crank

            @pl.when(jnp.logical_not(ok3))
            def _():
                sel_ref[...] = _extract_exact(cand, PEER_TOPK)

            sel = sel_ref[...] < float(PEER_TOPK)
            cexp = jnp.where(sel, jnp.exp(jnp.where(sel, cand - cand[0:1, :], 0.0)), 0.0)
            inv_z = 1.0 / jnp.sum(cexp, axis=0, keepdims=True)
            self32 = sel.astype(F32)
            rank1 = rk1_ref[...].astype(BF16)
            n1 = jnp.zeros(rank1.shape, BF16)
            n = 0
            for a in range(PEER_TOPK):
                width = PEER_TOPK // (a + 1)
                n_a = jnp.sum(self32[n:n + width, :], axis=0, keepdims=True)
                n1 = jnp.where(rank1 == jnp.asarray(a, BF16), _bcast_rows_bf16(n_a, N_KEYS), n1)
                n += width
            e1_ref[h] = jnp.exp(s1 - v1_ref[0:1, :]) * (0.5 * inv_z)
            n1_ref[h] = n1.astype(F32)
            r2_ref[h] = rk2_ref[...].astype(BF16)
            e2_ref[h] = jnp.exp(s2 - v2_ref[0:1, :]).astype(BF16)
            return carry

        lax.fori_loop(0, PEER_HEADS, head, 0)

    th = t // 2

    def stage(blk, hf, u_next_ref, h_next_ref, h_cur_ref, p_cur_ref, vt_prev_ref, p_prev_ref):
        lanes = slice(hf * th, (hf + 1) * th)
        h_next_ref[:, lanes] = jnp.dot(u_next_ref[...], xnt_ref[:, lanes], preferred_element_type=F32)
        for kk in range(PEER_IBLK):
            i = blk * PEER_IBLK + kk
            rows = slice(kk * N_KEYS, (kk + 1) * N_KEYS)
            hk = h_cur_ref[rows, lanes]
            act = hk * (1.0 + lax.erf(hk * 0.7071067811865476))
            g = jnp.zeros((N_KEYS, th), BF16)
            for h in range(PEER_HEADS):
                e1row = _bcast_rows_bf16(e1_ref[h, pl.ds(i, 1), lanes], N_KEYS)
                n1row = _bcast_rows_bf16(n1_ref[h, pl.ds(i, 1), lanes], N_KEYS)
                g = g + e1row * jnp.where(r2_ref[h, :, lanes] < n1row, e2_ref[h, :, lanes], jnp.zeros((), BF16))
            p_cur_ref[rows, lanes] = g * act.astype(BF16)
        yt_ref[:, lanes] += jnp.dot(vt_prev_ref[...], p_prev_ref[:, lanes], preferred_element_type=F32)

    for hf in range(2):
        stage(2 * ib, hf, u_odd_ref, h_odd_ref, h_even_ref, p_even_ref, vt_odd_ref, p_odd_ref)
    for hf in range(2):
        stage(2 * ib + 1, hf, u_even_ref, h_even_ref, h_odd_ref, p_odd_ref, vt_even_ref, p_even_ref)

    @pl.when(ib == nib - 1)
    def _():
        yt = yt_ref[...] + jnp.dot(vt_last_ref[...], p_odd_ref[...], preferred_element_type=F32)
        hfin = h1_ref[...] + yt.T
        out_ref[...] = _rms(hfin, fw_ref[...])


def _peer(xn2t, h1, wqt, k1, k2, u, vt, fw):
    n = xn2t.shape[1]
    t = PEER_TILE
    eb = PEER_IBLK * N_KEYS
    nblk = (N_KEYS * N_KEYS) // eb
    assert nblk % 2 == 0
    tok = pl.BlockSpec((t, D_MODEL), lambda i, j: (i, 0))
    big = pltpu.VMEM((PEER_HEADS, N_KEYS, t), F32)
    bigb = pltpu.VMEM((PEER_HEADS, N_KEYS, t), BF16)
    keys = pltpu.VMEM((N_KEYS, t), F32)
    u_spec = lambda f: pl.BlockSpec((eb, D_MODEL), lambda i, j: (f(j), 0))
    vt_spec = lambda f: pl.BlockSpec((D_MODEL, eb), lambda i, j: (0, f(j)))
    return pl.pallas_call(
        _peer_kernel,
        grid=(n // t, nblk // 2),
        in_specs=[pl.BlockSpec((D_MODEL, t), lambda i, j: (0, i)), tok,
                  _const_spec(wqt.shape), _const_spec(k1.shape), _const_spec(k2.shape),
                  u_spec(lambda j: 0), u_spec(lambda j: 2 * j + 1),
                  u_spec(lambda j: jnp.minimum(2 * j + 2, nblk - 1)),
                  vt_spec(lambda j: jnp.maximum(2 * j - 1, 0)), vt_spec(lambda j: 2 * j),
                  vt_spec(lambda j: nblk - 1),
                  _const_spec(fw.shape)],
        out_specs=tok,
        out_shape=jax.ShapeDtypeStruct((n, D_MODEL), F32),
        scratch_shapes=[big, big, bigb, bigb, keys, keys,
                        pltpu.VMEM((PEER_TOPK, t), F32), pltpu.VMEM((PEER_TOPK, t), F32),
                        pltpu.VMEM((PEER_CAND_ROWS, t), F32), pltpu.VMEM((PEER_CAND_ROWS, t), F32),
                        pltpu.VMEM((D_MODEL, t), F32),
                        pltpu.VMEM((eb, t), F32), pltpu.VMEM((eb, t), F32),
                        pltpu.VMEM((eb, t), BF16), pltpu.VMEM((eb, t), BF16),
                        pltpu.VMEM((2 * PEER_HALF * PEER_HEADS, t), BF16)],
        compiler_params=_params("parallel", "arbitrary"),
        name="peer",
    )(xn2t, h1, wqt, k1, k2, u, u, u, vt, vt, vt, fw)


def kernel(x, mix_norm_w, w_in, pool_w, pool_scale, conv_w, a_log, dt_bias, dn_norm_w, w_pool_up,
           w_dn_up, w_mix_out, ffn_norm_w, peer_w_query, peer_keys_1, peer_keys_2, peer_down, peer_up,
           final_norm_w):
    b, s, d = x.shape
    n = b * s
    assert d == D_MODEL and s % POOL_TILE == 0 and s % DN_CHUNK == 0
    assert n % IN_TILE == 0 and n % POST_TILE == 0 and n % PEER_TILE == 0
    assert w_in.shape[0] == 1, "single-layer block"
    l = 0
    h = x.reshape(n, d)

    c0 = POOL_WIDTH
    c1 = c0 + 3 * DN_WIDTH
    c2 = c1 + DN_WIDTH
    c3 = c2 + 2 * DN_HEADS
    c4 = c3 + D_MODEL
    wi = w_in[l]
    xa, qkv, z, ga, gb, bd = _in_proj(
        h, mix_norm_w[l].reshape(1, d),
        wi[:, :c0].astype(BF16), wi[:, c0:c1].astype(BF16), wi[:, c1:c2].astype(BF16),
        wi[:, c3:c4].astype(BF16), wi[:, c4:].astype(BF16), wi[:, c2:c3])

    ya = _pool(xa.reshape(b, s, POOL_WIDTH), pool_w[l].astype(BF16), pool_scale[l].reshape(1, POOL_WIDTH),
               w_pool_up[l].astype(BF16))

    bd3 = bd.reshape(b, s, 2 * DN_HEADS)
    o = _deltanet(qkv.reshape(b, s, 3 * DN_WIDTH), bd3, bd3.transpose(0, 2, 1), conv_w[l], a_log[l],
                  dt_bias[l])

    h1, xn2t = _post(o.reshape(n, DN_WIDTH), z, ga, gb, ya.reshape(n, d), h,
                    dn_norm_w[l].reshape(1, DN_HEAD_DIM), w_dn_up[l].astype(BF16),
                    w_mix_out[l].astype(BF16), ffn_norm_w[l].reshape(1, d))

    out = _peer(xn2t, h1, peer_w_query[l].T.astype(BF16), peer_keys_1[l].astype(BF16),
                peer_keys_2[l].astype(BF16), peer_down[l].astype(BF16), peer_up[l].T.astype(BF16),
                final_norm_w.reshape(1, d))
    return out.reshape(b, s, d)
```

```python
import functools

import jax
import jax.numpy as jnp
from jax import lax
from jax.experimental import pallas as pl
from jax.experimental.pallas import tpu as pltpu

F32 = jnp.float32
BF16 = jnp.bfloat16
EPS = 1e-6

D_MODEL = 1024
POOL_WINDOWS = (2, 4, 8, 16)
POOL_GROUP_DIM = 128
POOL_WIDTH = 512
POOL_HALO = 16
DN_HEADS = 8
DN_HEAD_DIM = 128
DN_WIDTH = DN_HEADS * DN_HEAD_DIM
CONV_WIDTH = 4
CONV_HALO = 8
DN_CHUNK = 128
PEER_HEADS = 8
N_KEYS = 128
PEER_TOPK = 16
PEER_HALF = 128

V7X_LANES = 128
BF16_SUBLANES = 16
VMEM_LIMIT = 56 * 1024 * 1024

IN_TILE = 512
POOL_TILE = 512
POST_TILE = 256
PEER_TILE = 512
PEER_IBLK = 4

PEER_CANDS = tuple((a, b) for a in range(PEER_TOPK) for b in range(PEER_TOPK)
                   if (a + 1) * (b + 1) <= PEER_TOPK)
PEER_CAND_ROWS = 56

_NT = (((1,), (1,)), ((), ()))
_TN = (((0,), (0,)), ((), ()))


def _params(*sem):
    return pltpu.CompilerParams(dimension_semantics=sem, vmem_limit_bytes=VMEM_LIMIT)


def _const_spec(shape):
    return pl.BlockSpec(shape, lambda *_: (0,) * len(shape))


def _resident_spec(shape):
    return pl.BlockSpec(shape, lambda *_: (0,) * len(shape), pipeline_mode=pl.Buffered(1))


def _sigmoid(x):
    return 1.0 / (1.0 + jnp.exp(-x))


def _in_proj_kernel(x_ref, nw_ref, wxa_ref, wqkv_ref, wz_ref, wga_ref, wgb_ref, wbd_ref,
                    xa_ref, qkv_ref, z_ref, ga_ref, gb_ref, bd_ref):
    x = x_ref[...]
    xn = x * lax.rsqrt(jnp.mean(x * x, axis=-1, keepdims=True) + EPS) * nw_ref[...]
    xb = xn.astype(BF16)
    xa_ref[...] = jnp.dot(xb, wxa_ref[...], preferred_element_type=F32)
    qkv_ref[...] = jnp.dot(xb, wqkv_ref[...], preferred_element_type=F32)
    z_ref[...] = jnp.dot(xb, wz_ref[...], preferred_element_type=F32).astype(BF16)
    ga_ref[...] = jnp.dot(xb, wga_ref[...], preferred_element_type=F32).astype(BF16)
    gb_ref[...] = jnp.dot(xb, wgb_ref[...], preferred_element_type=F32).astype(BF16)
    bd_ref[...] = jnp.dot(xb, wbd_ref[...], preferred_element_type=F32)


def _in_proj(x2, nw, wxa, wqkv, wz, wga, wgb, wbd):
    n = x2.shape[0]
    t = IN_TILE
    row = lambda w: pl.BlockSpec((t, w), lambda i: (i, 0))
    widths = (POOL_WIDTH, 3 * DN_WIDTH, DN_WIDTH, D_MODEL, D_MODEL, 2 * DN_HEADS)
    dtypes = (F32, F32, BF16, BF16, BF16, F32)
    return pl.pallas_call(
        _in_proj_kernel,
        grid=(n // t,),
        in_specs=[row(D_MODEL), _const_spec(nw.shape), _resident_spec(wxa.shape), _resident_spec(wqkv.shape),
                  _resident_spec(wz.shape), _resident_spec(wga.shape), _resident_spec(wgb.shape),
                  _const_spec(wbd.shape)],
        out_specs=[row(w) for w in widths],
        out_shape=[jax.ShapeDtypeStruct((n, w), dt) for w, dt in zip(widths, dtypes)],
        compiler_params=_params("parallel"),
        name="in_proj",
    )(x2, nw, wxa, wqkv, wz, wga, wgb, wbd)


def _pool_kernel(xa_ref, pw_ref, ps_ref, wup_ref, ya_ref, buf_ref):
    s = pl.program_id(1)
    t = xa_ref.shape[1]

    @pl.when(s == 0)
    def _():
        buf_ref[0:POOL_HALO, :] = jnp.zeros((POOL_HALO, POOL_WIDTH), F32)

    @pl.when(s > 0)
    def _():
        buf_ref[0:POOL_HALO, :] = buf_ref[t:t + POOL_HALO, :]

    buf_ref[POOL_HALO:, :] = xa_ref[0]
    pos = s * t + lax.broadcasted_iota(jnp.int32, (t, 1), 0)
    ys = []
    for g, win in enumerate(POOL_WINDOWS):
        sl = slice(g * POOL_GROUP_DIM, (g + 1) * POOL_GROUP_DIM)
        xg = buf_ref[POOL_HALO:POOL_HALO + t, sl]
        acc = xg
        for k in range(1, win):
            acc = acc + buf_ref[POOL_HALO - k:POOL_HALO - k + t, sl]
        cnt = jnp.minimum(pos + 1, win).astype(F32)
        pooled = acc / cnt - xg
        y = jnp.dot(pooled.astype(BF16), pw_ref[g], preferred_element_type=F32)
        ys.append(y * ps_ref[:, sl])
    y = jnp.concatenate(ys, axis=-1)
    ya_ref[0] = jnp.dot(y.astype(BF16), wup_ref[...], preferred_element_type=F32).astype(BF16)


def _pool(xa, pw, ps, wup):
    b, s, _ = xa.shape
    t = POOL_TILE
    return pl.pallas_call(
        _pool_kernel,
        grid=(b, s // t),
        in_specs=[pl.BlockSpec((1, t, POOL_WIDTH), lambda i, j: (i, j, 0)),
                  _const_spec(pw.shape), _const_spec(ps.shape), _const_spec(wup.shape)],
        out_specs=pl.BlockSpec((1, t, D_MODEL), lambda i, j: (i, j, 0)),
        out_shape=jax.ShapeDtypeStruct((b, s, D_MODEL), BF16),
        scratch_shapes=[pltpu.VMEM((t + POOL_HALO, POOL_WIDTH), F32)],
        compiler_params=_params("parallel", "arbitrary"),
        name="pool",
    )(xa, pw, ps, wup)


def _softplus(x):
    return jnp.maximum(x, 0.0) + jnp.log1p(jnp.exp(-jnp.abs(x)))


def _dn_kernel(qkv_ref, bd_ref, bdt_ref, cw_ref, alog_ref, dtb_ref, alogt_ref, dtbt_ref,
               o_ref, cbuf_ref, st_ref):
    s = pl.program_id(1)
    c = DN_CHUNK
    hd = DN_HEAD_DIM

    @pl.when(s == 0)
    def _():
        cbuf_ref[0:CONV_HALO, :] = jnp.zeros((CONV_HALO, 3 * DN_WIDTH), F32)
        st_ref[...] = jnp.zeros(st_ref.shape, F32)

    @pl.when(s > 0)
    def _():
        cbuf_ref[0:CONV_HALO, :] = cbuf_ref[c:c + CONV_HALO, :]

    cbuf_ref[CONV_HALO:, :] = qkv_ref[0]

    bd = bd_ref[0]
    bdt = bdt_ref[0]
    beta_all = _sigmoid(bd[:, 0:DN_HEADS])
    g_all = -jnp.exp(alog_ref[...]) * _softplus(bd[:, DN_HEADS:] + dtb_ref[...])
    g_allt = -jnp.exp(alogt_ref[...]) * _softplus(bdt[DN_HEADS:, :] + dtbt_ref[...])

    row = lax.broadcasted_iota(jnp.int32, (c, c), 0)
    col = lax.broadcasted_iota(jnp.int32, (c, c), 1)
    tril = row >= col
    strict = row > col
    gc = jnp.dot(tril.astype(F32), g_all, preferred_element_type=F32, precision=lax.Precision.HIGHEST)
    gct = jnp.dot(g_allt, (row <= col).astype(F32), preferred_element_type=F32,
                  precision=lax.Precision.HIGHEST)
    eye = (row == col).astype(F32)

    def conv_silu(lane0):
        sl = slice(lane0, lane0 + hd)
        y = cw_ref[0:1, sl] * cbuf_ref[CONV_HALO - 3:CONV_HALO - 3 + c, sl]
        for j in range(1, CONV_WIDTH):
            y = y + cw_ref[j:j + 1, sl] * cbuf_ref[CONV_HALO - 3 + j:CONV_HALO - 3 + j + c, sl]
        return y * _sigmoid(y)

    def l2n(v):
        return v * lax.rsqrt(jnp.sum(v * v, axis=-1, keepdims=True) + EPS)

    hs = range(DN_HEADS)
    q = [l2n(conv_silu(h * hd)) * (hd ** -0.5) for h in hs]
    k = [l2n(conv_silu(DN_WIDTH + h * hd)) for h in hs]
    v = [conv_silu(2 * DN_WIDTH + h * hd) for h in hs]
    gcol = [gc[:, h:h + 1] for h in hs]
    grow = [gct[h:h + 1, :] for h in hs]
    decay = [jnp.where(tril, jnp.exp(jnp.where(tril, gcol[h] - grow[h], 0.0)), 0.0) for h in hs]
    kb = [k[h] * beta_all[:, h:h + 1] for h in hs]
    vb = [v[h] * beta_all[:, h:h + 1] for h in hs]
    kf = [k[h].astype(BF16) for h in hs]
    lower = [jnp.where(strict, lax.dot_general(kb[h].astype(BF16), kf[h], _NT, preferred_element_type=F32)
                       * decay[h], 0.0) for h in hs]
    attn = [(lax.dot_general(q[h].astype(BF16), kf[h], _NT, preferred_element_type=F32) * decay[h]).astype(BF16)
            for h in hs]
    pair = ((row ^ col) == 1) & strict
    tinv = [eye - jnp.where(pair, lower[h], 0.0) for h in hs]
    blk = 2
    while blk < c:
        m = ((row ^ col) < 2 * blk) & ((row & blk) != 0) & ((col & blk) == 0)
        tb = [tinv[h].astype(BF16) for h in hs]
        left = [jnp.dot(tb[h], jnp.where(m, lower[h], 0.0).astype(BF16), preferred_element_type=F32).astype(BF16)
                for h in hs]
        tinv = [tinv[h] - jnp.dot(left[h], tb[h], preferred_element_type=F32) for h in hs]
        blk *= 2
    eg = [jnp.exp(gcol[h]) for h in hs]
    uw = [jnp.dot(tinv[h].astype(BF16),
                  jnp.concatenate([vb[h], kb[h] * eg[h]], axis=-1).astype(BF16), preferred_element_type=F32)
          for h in hs]
    st = [st_ref[h] for h in hs]
    ws = [jnp.dot(jnp.concatenate([uw[h][:, hd:], q[h] * eg[h]], axis=0).astype(BF16), st[h].astype(BF16),
                  preferred_element_type=F32) for h in hs]
    vnb = [(uw[h][:, :hd] - ws[h][:c]).astype(BF16) for h in hs]
    o = [ws[h][c:] + jnp.dot(attn[h], vnb[h], preferred_element_type=F32) for h in hs]
    for h in hs:
        g_last = grow[h][:, c - 1:c]
        k_dec = (k[h] * jnp.exp(g_last - gcol[h])).astype(BF16)
        st_ref[h] = st[h] * jnp.exp(g_last) + lax.dot_general(k_dec, vnb[h], _TN, preferred_element_type=F32)
    o_ref[0] = jnp.concatenate(o, axis=-1).astype(BF16)


def _deltanet(qkv, bd, bdt, cw, alog, dtb):
    b, s, _ = qkv.shape
    c = DN_CHUNK
    alog2, dtb2 = alog.reshape(1, DN_HEADS), dtb.reshape(1, DN_HEADS)
    alogt, dtbt = alog.reshape(DN_HEADS, 1), dtb.reshape(DN_HEADS, 1)
    return pl.pallas_call(
        _dn_kernel,
        grid=(b, s // c),
        in_specs=[pl.BlockSpec((1, c, 3 * DN_WIDTH), lambda i, j: (i, j, 0)),
                  pl.BlockSpec((1, c, 2 * DN_HEADS), lambda i, j: (i, j, 0)),
                  pl.BlockSpec((1, 2 * DN_HEADS, c), lambda i, j: (i, 0, j)),
                  _const_spec(cw.shape), _const_spec(alog2.shape), _const_spec(dtb2.shape),
                  _const_spec(alogt.shape), _const_spec(dtbt.shape)],
        out_specs=pl.BlockSpec((1, c, DN_WIDTH), lambda i, j: (i, j, 0)),
        out_shape=jax.ShapeDtypeStruct((b, s, DN_WIDTH), BF16),
        scratch_shapes=[pltpu.VMEM((c + CONV_HALO, 3 * DN_WIDTH), F32),
                        pltpu.VMEM((DN_HEADS, DN_HEAD_DIM, DN_HEAD_DIM), F32)],
        compiler_params=_params("parallel", "arbitrary"),
        name="deltanet",
    )(qkv, bd, bdt, cw, alog2, dtb2, alogt, dtbt)


def _rms(x, w):
    return x * lax.rsqrt(jnp.mean(x * x, axis=-1, keepdims=True) + EPS) * w


def _post_kernel(o_ref, z_ref, ga_ref, gb_ref, ya_ref, x_ref, dnw_ref, wdn_ref, wmix_ref, fnw_ref,
                 h1_ref, xn2t_ref):
    o = o_ref[...].astype(F32)
    z = z_ref[...].astype(F32)
    parts = []
    for h in range(DN_HEADS):
        sl = slice(h * DN_HEAD_DIM, (h + 1) * DN_HEAD_DIM)
        parts.append(_rms(o[:, sl], dnw_ref[...]))
    on = jnp.concatenate(parts, axis=-1) * (z * _sigmoid(z))
    yb = jnp.dot(on.astype(BF16), wdn_ref[...], preferred_element_type=F32)
    merged = (_sigmoid(ga_ref[...].astype(F32)) * ya_ref[...].astype(F32)
              + _sigmoid(gb_ref[...].astype(F32)) * yb)
    h1 = x_ref[...] + jnp.dot(merged.astype(BF16), wmix_ref[...], preferred_element_type=F32)
    h1_ref[...] = h1
    xn2t_ref[...] = _rms(h1, fnw_ref[...]).T.astype(BF16)


def _post(o, z, ga, gb, ya, x2, dnw, wdn, wmix, fnw):
    n = x2.shape[0]
    t = POST_TILE
    row = pl.BlockSpec((t, D_MODEL), lambda i: (i, 0))
    return pl.pallas_call(
        _post_kernel,
        grid=(n // t,),
        in_specs=[row, row, row, row, row, row, _const_spec(dnw.shape), _const_spec(wdn.shape),
                  _const_spec(wmix.shape), _const_spec(fnw.shape)],
        out_specs=[row, pl.BlockSpec((D_MODEL, t), lambda i: (0, i))],
        out_shape=[jax.ShapeDtypeStruct((n, D_MODEL), F32), jax.ShapeDtypeStruct((D_MODEL, n), BF16)],
        compiler_params=_params("parallel"),
        name="post",
    )(o, z, ga, gb, ya, x2, dnw, wdn, wmix, fnw)


_MARK_BASE = -2.0 ** 120
_MARK_STEP = 1.0 / 32.0
_FAST_FLOOR = -2.0 ** 119


def _extract_fast(x, rounds, vals_ref=None):
    ranks, bad = [], []
    for l0 in range(0, x.shape[1], V7X_LANES):
        lanes = slice(l0, l0 + V7X_LANES)
        work = x[:, lanes]
        too_low = jnp.min(work, axis=0, keepdims=True) < _FAST_FLOOR
        for r in range(rounds):
            m = jnp.max(work, axis=0, keepdims=True)
            if vals_ref is not None:
                vals_ref[r:r + 1, lanes] = m
            work = jnp.where(work == m, _MARK_BASE * (1.0 + r * _MARK_STEP), work)
        hit = work <= _MARK_BASE
        ranks.append(jnp.where(hit, (work * (1.0 / _MARK_BASE) - 1.0) * (1.0 / _MARK_STEP), float(rounds)))
        miscount = jnp.abs(jnp.sum(hit.astype(F32), axis=0, keepdims=True) - float(rounds))
        bad.append(jnp.where(too_low, 1.0, miscount))
    ok = jnp.max(jnp.concatenate(bad, axis=1)) == 0.0
    return jnp.concatenate(ranks, axis=1), ok


def _extract_exact(work, rounds, vals_ref=None):
    rows = work.shape[0]
    iota = lax.broadcasted_iota(jnp.int32, work.shape, 0).astype(F32)
    rank = jnp.full(work.shape, float(rounds), F32)
    for r in range(rounds):
        m = jnp.max(work, axis=0, keepdims=True)
        idx = jnp.min(jnp.where(work == m, iota, float(rows)), axis=0, keepdims=True)
        hit = iota == idx
        rank = jnp.where(hit, float(r), rank)
        work = jnp.where(hit, -jnp.inf, work)
        if vals_ref is not None:
            vals_ref[r:r + 1, :] = m
    return rank


def _bcast_rows_bf16(row, rows):
    packed = jnp.broadcast_to(row, (BF16_SUBLANES, row.shape[1])).astype(BF16)
    return jnp.concatenate([packed] * (rows // BF16_SUBLANES), axis=0)


def _peer_kernel(xnt_ref, h1_ref, wq_ref, k1_ref, k2_ref, u_first_ref, u_odd_ref, u_even_ref,
                 vt_odd_ref, vt_even_ref, vt_last_ref, fw_ref, out_ref,
                 e1_ref, n1_ref, r2_ref, e2_ref, rk1_ref, rk2_ref, v1_ref, v2_ref, cand_ref, sel_ref, yt_ref,
                 h_even_ref, h_odd_ref, p_even_ref, p_odd_ref, q_ref):
    ib = pl.program_id(1)
    nib = pl.num_programs(1)
    t = xnt_ref.shape[1]

    @pl.when(ib == 0)
    def _():
        yt_ref[...] = jnp.zeros(yt_ref.shape, F32)
        p_odd_ref[...] = jnp.zeros(p_odd_ref.shape, BF16)
        xnt = xnt_ref[...]
        h_even_ref[...] = jnp.dot(u_first_ref[...], xnt, preferred_element_type=F32)
        q_ref[...] = jnp.dot(wq_ref[...], xnt, preferred_element_type=F32).astype(BF16)

        def head(h, carry):
            r0 = pl.multiple_of(h * 2 * PEER_HALF, 2 * PEER_HALF)
            s1 = jnp.dot(k1_ref[h], q_ref[pl.ds(r0, PEER_HALF), :], preferred_element_type=F32)
            s2 = jnp.dot(k2_ref[h], q_ref[pl.ds(r0 + PEER_HALF, PEER_HALF), :], preferred_element_type=F32)
            rank1, ok1 = _extract_fast(s1, PEER_TOPK, v1_ref)
            rank2, ok2 = _extract_fast(s2, PEER_TOPK, v2_ref)
            rk1_ref[...] = rank1
            rk2_ref[...] = rank2

            @pl.when(jnp.logical_not(ok1 & ok2))
            def _():
                rk1_ref[...] = _extract_exact(s1, PEER_TOPK, v1_ref)
                rk2_ref[...] = _extract_exact(s2, PEER_TOPK, v2_ref)

            cand_ref[...] = jnp.full(cand_ref.shape, -jnp.inf, F32)
            for n, (a, b) in enumerate(PEER_CANDS):
                cand_ref[n:n + 1, :] = v1_ref[a:a + 1, :] + v2_ref[b:b + 1, :]
            cand = cand_ref[...]
            valid = lax.broadcasted_iota(jnp.int32, cand.shape, 0) < len(PEER_CANDS)
            crank, ok3 = _extract_fast(jnp.where(valid, cand, _FAST_FLOOR), PEER_TOPK)
            sel_ref[...] = crank

            @pl.when(jnp.logical_not(ok3))
            def _():
                sel_ref[...] = _extract_exact(cand, PEER_TOPK)

            sel = sel_ref[...] < float(PEER_TOPK)
            cexp = jnp.where(sel, jnp.exp(jnp.where(sel, cand - cand[0:1, :], 0.0)), 0.0)
            inv_z = 1.0 / jnp.sum(cexp, axis=0, keepdims=True)
            self32 = sel.astype(F32)
            rank1 = rk1_ref[...].astype(BF16)
            n1 = jnp.zeros(rank1.shape, BF16)
            n = 0
            for a in range(PEER_TOPK):
                width = PEER_TOPK // (a + 1)
                n_a = jnp.sum(self32[n:n + width, :], axis=0, keepdims=True)
                n1 = jnp.where(rank1 == jnp.asarray(a, BF16), _bcast_rows_bf16(n_a, N_KEYS), n1)
                n += width
            e1_ref[h] = jnp.exp(s1 - v1_ref[0:1, :]) * (0.5 * inv_z)
            n1_ref[h] = n1.astype(F32)
            r2_ref[h] = rk2_ref[...].astype(BF16)
            e2_ref[h] = jnp.exp(s2 - v2_ref[0:1, :]).astype(BF16)
            return carry

        lax.fori_loop(0, PEER_HEADS, head, 0)

    th = t // 2

    def stage(blk, hf, u_next_ref, h_next_ref, h_cur_ref, p_cur_ref, vt_prev_ref, p_prev_ref):
        lanes = slice(hf * th, (hf + 1) * th)
        h_next_ref[:, lanes] = jnp.dot(u_next_ref[...], xnt_ref[:, lanes], preferred_element_type=F32)
        for kk in range(PEER_IBLK):
            i = blk * PEER_IBLK + kk
            rows = slice(kk * N_KEYS, (kk + 1) * N_KEYS)
            hk = h_cur_ref[rows, lanes]
            act = hk * (1.0 + lax.erf(hk * 0.7071067811865476))
            g = jnp.zeros((N_KEYS, th), BF16)
            for h in range(PEER_HEADS):
                e1row = _bcast_rows_bf16(e1_ref[h, pl.ds(i, 1), lanes], N_KEYS)
                n1row = _bcast_rows_bf16(n1_ref[h, pl.ds(i, 1), lanes], N_KEYS)
                g = g + e1row * jnp.where(r2_ref[h, :, lanes] < n1row, e2_ref[h, :, lanes], jnp.zeros((), BF16))
            p_cur_ref[rows, lanes] = g * act.astype(BF16)
        yt_ref[:, lanes] += jnp.dot(vt_prev_ref[...], p_prev_ref[:, lanes], preferred_element_type=F32)

    for hf in range(2):
        stage(2 * ib, hf, u_odd_ref, h_odd_ref, h_even_ref, p_even_ref, vt_odd_ref, p_odd_ref)
    for hf in range(2):
        stage(2 * ib + 1, hf, u_even_ref, h_even_ref, h_odd_ref, p_odd_ref, vt_even_ref, p_even_ref)

    @pl.when(ib == nib - 1)
    def _():
        yt = yt_ref[...] + jnp.dot(vt_last_ref[...], p_odd_ref[...], preferred_element_type=F32)
        hfin = h1_ref[...] + yt.T
        out_ref[...] = _rms(hfin, fw_ref[...])


def _peer(xn2t, h1, wqt, k1, k2, u, vt, fw):
    n = xn2t.shape[1]
    t = PEER_TILE
    eb = PEER_IBLK * N_KEYS
    nblk = (N_KEYS * N_KEYS) // eb
    assert nblk % 2 == 0
    tok = pl.BlockSpec((t, D_MODEL), lambda i, j: (i, 0))
    big = pltpu.VMEM((PEER_HEADS, N_KEYS, t), F32)
    bigb = pltpu.VMEM((PEER_HEADS, N_KEYS, t), BF16)
    keys = pltpu.VMEM((N_KEYS, t), F32)
    u_spec = lambda f: pl.BlockSpec((eb, D_MODEL), lambda i, j: (f(j), 0))
    vt_spec = lambda f: pl.BlockSpec((D_MODEL, eb), lambda i, j: (0, f(j)))
    return pl.pallas_call(
        _peer_kernel,
        grid=(n // t, nblk // 2),
        in_specs=[pl.BlockSpec((D_MODEL, t), lambda i, j: (0, i)), tok,
                  _const_spec(wqt.shape), _const_spec(k1.shape), _const_spec(k2.shape),
                  u_spec(lambda j: 0), u_spec(lambda j: 2 * j + 1),
                  u_spec(lambda j: jnp.minimum(2 * j + 2, nblk - 1)),
                  vt_spec(lambda j: jnp.maximum(2 * j - 1, 0)), vt_spec(lambda j: 2 * j),
                  vt_spec(lambda j: nblk - 1),
                  _const_spec(fw.shape)],
        out_specs=tok,
        out_shape=jax.ShapeDtypeStruct((n, D_MODEL), F32),
        scratch_shapes=[big, big, bigb, bigb, keys, keys,
                        pltpu.VMEM((PEER_TOPK, t), F32), pltpu.VMEM((PEER_TOPK, t), F32),
                        pltpu.VMEM((PEER_CAND_ROWS, t), F32), pltpu.VMEM((PEER_CAND_ROWS, t), F32),
                        pltpu.VMEM((D_MODEL, t), F32),
                        pltpu.VMEM((eb, t), F32), pltpu.VMEM((eb, t), F32),
                        pltpu.VMEM((eb, t), BF16), pltpu.VMEM((eb, t), BF16),
                        pltpu.VMEM((2 * PEER_HALF * PEER_HEADS, t), BF16)],
        compiler_params=_params("parallel", "arbitrary"),
        name="peer",
    )(xn2t, h1, wqt, k1, k2, u, u, u, vt, vt, vt, fw)


def kernel(x, mix_norm_w, w_in, pool_w, pool_scale, conv_w, a_log, dt_bias, dn_norm_w, w_pool_up,
           w_dn_up, w_mix_out, ffn_norm_w, peer_w_query, peer_keys_1, peer_keys_2, peer_down, peer_up,
           final_norm_w):
    b, s, d = x.shape
    n = b * s
    assert d == D_MODEL and s % POOL_TILE == 0 and s % DN_CHUNK == 0
    assert n % IN_TILE == 0 and n % POST_TILE == 0 and n % PEER_TILE == 0
    assert w_in.shape[0] == 1, "single-layer block"
    l = 0
    h = x.reshape(n, d)

    c0 = POOL_WIDTH
    c1 = c0 + 3 * DN_WIDTH
    c2 = c1 + DN_WIDTH
    c3 = c2 + 2 * DN_HEADS
    c4 = c3 + D_MODEL
    wi = w_in[l]
    xa, qkv, z, ga, gb, bd = _in_proj(
        h, mix_norm_w[l].reshape(1, d),
        wi[:, :c0].astype(BF16), wi[:, c0:c1].astype(BF16), wi[:, c1:c2].astype(BF16),
        wi[:, c3:c4].astype(BF16), wi[:, c4:].astype(BF16), wi[:, c2:c3].astype(BF16))

    ya = _pool(xa.reshape(b, s, POOL_WIDTH), pool_w[l].astype(BF16), pool_scale[l].reshape(1, POOL_WIDTH),
               w_pool_up[l].astype(BF16))

    bd3 = bd.reshape(b, s, 2 * DN_HEADS)
    o = _deltanet(qkv.reshape(b, s, 3 * DN_WIDTH), bd3, bd3.transpose(0, 2, 1), conv_w[l], a_log[l],
                  dt_bias[l])

    h1, xn2t = _post(o.reshape(n, DN_WIDTH), z, ga, gb, ya.reshape(n, d), h,
                    dn_norm_w[l].reshape(1, DN_HEAD_DIM), w_dn_up[l].astype(BF16),
                    w_mix_out[l].astype(BF16), ffn_norm_w[l].reshape(1, d))

    out = _peer(xn2t, h1, peer_w_query[l].T.astype(BF16), peer_keys_1[l].astype(BF16),
                peer_keys_2[l].astype(BF16), peer_down[l].astype(BF16), peer_up[l].T.astype(BF16),
                final_norm_w.reshape(1, d))
    return out.reshape(b, s, d)
```

```python
import functools

import jax
import jax.numpy as jnp
from jax import lax
from jax.experimental import pallas as pl
from jax.experimental.pallas import tpu as pltpu

F32 = jnp.float32
BF16 = jnp.bfloat16
EPS = 1e-6

D_MODEL = 1024
POOL_WINDOWS = (2, 4, 8, 16)
POOL_GROUP_DIM = 128
POOL_WIDTH = 512
POOL_HALO = 16
DN_HEADS = 8
DN_HEAD_DIM = 128
DN_WIDTH = DN_HEADS * DN_HEAD_DIM
CONV_WIDTH = 4
CONV_HALO = 8
DN_STEP = 512
DN_CHUNK = 128
PEER_HEADS = 8
N_KEYS = 128
PEER_TOPK = 16
PEER_HALF = 128

V7X_LANES = 128
BF16_SUBLANES = 16
VMEM_LIMIT = 56 * 1024 * 1024

IN_TILE = 512
POOL_TILE = 512
POST_TILE = 512
PEER_TILE = 512
PEER_IBLK = 4

PEER_CANDS = tuple((a, b) for a in range(PEER_TOPK) for b in range(PEER_TOPK)
                   if (a + 1) * (b + 1) <= PEER_TOPK)
PEER_CAND_ROWS = 56

_NT = (((1,), (1,)), ((), ()))
_TN = (((0,), (0,)), ((), ()))


def _params(*sem):
    return pltpu.CompilerParams(dimension_semantics=sem, vmem_limit_bytes=VMEM_LIMIT)


def _const_spec(shape):
    return pl.BlockSpec(shape, lambda *_: (0,) * len(shape))


def _resident_spec(shape):
    return pl.BlockSpec(shape, lambda *_: (0,) * len(shape), pipeline_mode=pl.Buffered(1))


def _sigmoid(x):
    return 1.0 / (1.0 + jnp.exp(-x))


def _in_proj_kernel(x_ref, nw_ref, wxa_ref, wqkv_ref, wz_ref, wga_ref, wgb_ref, wbd_ref,
                    xa_ref, qkv_ref, z_ref, ga_ref, gb_ref, bd_ref):
    x = x_ref[...]
    xn = x * lax.rsqrt(jnp.mean(x * x, axis=-1, keepdims=True) + EPS) * nw_ref[...]
    xb = xn.astype(BF16)
    xa_ref[...] = jnp.dot(xb, wxa_ref[...], preferred_element_type=F32)
    qkv_ref[...] = jnp.dot(xb, wqkv_ref[...], preferred_element_type=F32)
    z_ref[...] = jnp.dot(xb, wz_ref[...], preferred_element_type=F32).astype(BF16)
    ga_ref[...] = jnp.dot(xb, wga_ref[...], preferred_element_type=F32).astype(BF16)
    gb_ref[...] = jnp.dot(xb, wgb_ref[...], preferred_element_type=F32).astype(BF16)
    bd_ref[...] = jnp.dot(xb, wbd_ref[...], preferred_element_type=F32)


def _in_proj(x2, nw, wxa, wqkv, wz, wga, wgb, wbd):
    n = x2.shape[0]
    t = IN_TILE
    row = lambda w: pl.BlockSpec((t, w), lambda i: (i, 0))
    widths = (POOL_WIDTH, 3 * DN_WIDTH, DN_WIDTH, D_MODEL, D_MODEL, 2 * DN_HEADS)
    dtypes = (F32, F32, BF16, BF16, BF16, F32)
    return pl.pallas_call(
        _in_proj_kernel,
        grid=(n // t,),
        in_specs=[row(D_MODEL), _const_spec(nw.shape), _resident_spec(wxa.shape), _resident_spec(wqkv.shape),
                  _resident_spec(wz.shape), _resident_spec(wga.shape), _resident_spec(wgb.shape),
                  _const_spec(wbd.shape)],
        out_specs=[row(w) for w in widths],
        out_shape=[jax.ShapeDtypeStruct((n, w), dt) for w, dt in zip(widths, dtypes)],
        compiler_params=_params("parallel"),
        name="in_proj",
    )(x2, nw, wxa, wqkv, wz, wga, wgb, wbd)


def _pool_kernel(xa_ref, pw_ref, ps_ref, wup_ref, ya_ref, buf_ref):
    s = pl.program_id(1)
    t = xa_ref.shape[1]

    @pl.when(s == 0)
    def _():
        buf_ref[0:POOL_HALO, :] = jnp.zeros((POOL_HALO, POOL_WIDTH), F32)

    @pl.when(s > 0)
    def _():
        buf_ref[0:POOL_HALO, :] = buf_ref[t:t + POOL_HALO, :]

    buf_ref[POOL_HALO:, :] = xa_ref[0]
    pos = s * t + lax.broadcasted_iota(jnp.int32, (t, 1), 0)
    ys = []
    for g, win in enumerate(POOL_WINDOWS):
        sl = slice(g * POOL_GROUP_DIM, (g + 1) * POOL_GROUP_DIM)
        xg = buf_ref[POOL_HALO:POOL_HALO + t, sl]
        acc = xg
        for k in range(1, win):
            acc = acc + buf_ref[POOL_HALO - k:POOL_HALO - k + t, sl]
        cnt = jnp.minimum(pos + 1, win).astype(F32)
        pooled = acc / cnt - xg
        y = jnp.dot(pooled.astype(BF16), pw_ref[g], preferred_element_type=F32)
        ys.append(y * ps_ref[:, sl])
    y = jnp.concatenate(ys, axis=-1)
    ya_ref[0] = jnp.dot(y.astype(BF16), wup_ref[...], preferred_element_type=F32).astype(BF16)


def _pool(xa, pw, ps, wup):
    b, s, _ = xa.shape
    t = POOL_TILE
    return pl.pallas_call(
        _pool_kernel,
        grid=(b, s // t),
        in_specs=[pl.BlockSpec((1, t, POOL_WIDTH), lambda i, j: (i, j, 0)),
                  _const_spec(pw.shape), _const_spec(ps.shape), _const_spec(wup.shape)],
        out_specs=pl.BlockSpec((1, t, D_MODEL), lambda i, j: (i, j, 0)),
        out_shape=jax.ShapeDtypeStruct((b, s, D_MODEL), BF16),
        scratch_shapes=[pltpu.VMEM((t + POOL_HALO, POOL_WIDTH), F32)],
        compiler_params=_params("parallel", "arbitrary"),
        name="pool",
    )(xa, pw, ps, wup)


def _softplus(x):
    return jnp.maximum(x, 0.0) + jnp.log1p(jnp.exp(-jnp.abs(x)))


def _dn_kernel(qkv_ref, bd_ref, bdt_ref, cw_ref, alog_ref, dtb_ref, alogt_ref, dtbt_ref,
               o_ref, cbuf_ref, st_ref):
    s = pl.program_id(1)
    c = DN_CHUNK
    hd = DN_HEAD_DIM

    @pl.when(s == 0)
    def _():
        cbuf_ref[0:CONV_HALO, :] = jnp.zeros((CONV_HALO, 3 * DN_WIDTH), F32)
        st_ref[...] = jnp.zeros(st_ref.shape, F32)

    @pl.when(s > 0)
    def _():
        cbuf_ref[0:CONV_HALO, :] = cbuf_ref[DN_STEP:DN_STEP + CONV_HALO, :]

    cbuf_ref[CONV_HALO:, :] = qkv_ref[0]
    for r0 in range(0, DN_STEP, c):
        _dn_chunk(r0, bd_ref, bdt_ref, cw_ref, alog_ref, dtb_ref, alogt_ref, dtbt_ref, o_ref, cbuf_ref, st_ref)


def _dn_chunk(r0, bd_ref, bdt_ref, cw_ref, alog_ref, dtb_ref, alogt_ref, dtbt_ref, o_ref, cbuf_ref, st_ref):
    c = DN_CHUNK
    hd = DN_HEAD_DIM
    bd = bd_ref[0, r0:r0 + c, :]
    bdt = bdt_ref[0, :, r0:r0 + c]
    beta_all = _sigmoid(bd[:, 0:DN_HEADS])
    g_all = -jnp.exp(alog_ref[...]) * _softplus(bd[:, DN_HEADS:] + dtb_ref[...])
    g_allt = -jnp.exp(alogt_ref[...]) * _softplus(bdt[DN_HEADS:, :] + dtbt_ref[...])

    row = lax.broadcasted_iota(jnp.int32, (c, c), 0)
    col = lax.broadcasted_iota(jnp.int32, (c, c), 1)
    tril = row >= col
    strict = row > col
    gc = jnp.dot(tril.astype(F32), g_all, preferred_element_type=F32, precision=lax.Precision.HIGHEST)
    gct = jnp.dot(g_allt, (row <= col).astype(F32), preferred_element_type=F32,
                  precision=lax.Precision.HIGHEST)
    eye = (row == col).astype(F32)

    def conv_silu(lane0):
        sl = slice(lane0, lane0 + hd)
        y = cw_ref[0:1, sl] * cbuf_ref[r0 + CONV_HALO - 3:r0 + CONV_HALO - 3 + c, sl]
        for j in range(1, CONV_WIDTH):
            y = y + cw_ref[j:j + 1, sl] * cbuf_ref[r0 + CONV_HALO - 3 + j:r0 + CONV_HALO - 3 + j + c, sl]
        return y * _sigmoid(y)

    def l2n(v):
        return v * lax.rsqrt(jnp.sum(v * v, axis=-1, keepdims=True) + EPS)

    hs = range(DN_HEADS)
    q = [l2n(conv_silu(h * hd)) * (hd ** -0.5) for h in hs]
    k = [l2n(conv_silu(DN_WIDTH + h * hd)) for h in hs]
    v = [conv_silu(2 * DN_WIDTH + h * hd) for h in hs]
    gcol = [gc[:, h:h + 1] for h in hs]
    grow = [gct[h:h + 1, :] for h in hs]
    decay = [jnp.where(tril, jnp.exp(jnp.where(tril, gcol[h] - grow[h], 0.0)), 0.0) for h in hs]
    kb = [k[h] * beta_all[:, h:h + 1] for h in hs]
    vb = [v[h] * beta_all[:, h:h + 1] for h in hs]
    kf = [k[h].astype(BF16) for h in hs]
    lower = [jnp.where(strict, lax.dot_general(kb[h].astype(BF16), kf[h], _NT, preferred_element_type=F32)
                       * decay[h], 0.0) for h in hs]
    attn = [(lax.dot_general(q[h].astype(BF16), kf[h], _NT, preferred_element_type=F32) * decay[h]).astype(BF16)
            for h in hs]
    pair = ((row ^ col) == 1) & strict
    tinv = [eye - jnp.where(pair, lower[h], 0.0) for h in hs]
    blk = 2
    while blk < c:
        m = ((row ^ col) < 2 * blk) & ((row & blk) != 0) & ((col & blk) == 0)
        tb = [tinv[h].astype(BF16) for h in hs]
        left = [jnp.dot(tb[h], jnp.where(m, lower[h], 0.0).astype(BF16), preferred_element_type=F32).astype(BF16)
                for h in hs]
        tinv = [tinv[h] - jnp.dot(left[h], tb[h], preferred_element_type=F32) for h in hs]
        blk *= 2
    eg = [jnp.exp(gcol[h]) for h in hs]
    uw = [jnp.dot(tinv[h].astype(BF16),
                  jnp.concatenate([vb[h], kb[h] * eg[h]], axis=-1).astype(BF16), preferred_element_type=F32)
          for h in hs]
    st = [st_ref[h] for h in hs]
    ws = [jnp.dot(jnp.concatenate([uw[h][:, hd:], q[h] * eg[h]], axis=0).astype(BF16), st[h].astype(BF16),
                  preferred_element_type=F32) for h in hs]
    vnb = [(uw[h][:, :hd] - ws[h][:c]).astype(BF16) for h in hs]
    o = [ws[h][c:] + jnp.dot(attn[h], vnb[h], preferred_element_type=F32) for h in hs]
    for h in hs:
        g_last = grow[h][:, c - 1:c]
        k_dec = (k[h] * jnp.exp(g_last - gcol[h])).astype(BF16)
        st_ref[h] = st[h] * jnp.exp(g_last) + lax.dot_general(k_dec, vnb[h], _TN, preferred_element_type=F32)
    o_ref[0, r0:r0 + c, :] = jnp.concatenate(o, axis=-1).astype(BF16)


def _deltanet(qkv, bd, bdt, cw, alog, dtb):
    b, s, _ = qkv.shape
    c = DN_STEP
    alog2, dtb2 = alog.reshape(1, DN_HEADS), dtb.reshape(1, DN_HEADS)
    alogt, dtbt = alog.reshape(DN_HEADS, 1), dtb.reshape(DN_HEADS, 1)
    return pl.pallas_call(
        _dn_kernel,
        grid=(b, s // c),
        in_specs=[pl.BlockSpec((1, c, 3 * DN_WIDTH), lambda i, j: (i, j, 0)),
                  pl.BlockSpec((1, c, 2 * DN_HEADS), lambda i, j: (i, j, 0)),
                  pl.BlockSpec((1, 2 * DN_HEADS, c), lambda i, j: (i, 0, j)),
                  _const_spec(cw.shape), _const_spec(alog2.shape), _const_spec(dtb2.shape),
                  _const_spec(alogt.shape), _const_spec(dtbt.shape)],
        out_specs=pl.BlockSpec((1, c, DN_WIDTH), lambda i, j: (i, j, 0)),
        out_shape=jax.ShapeDtypeStruct((b, s, DN_WIDTH), BF16),
        scratch_shapes=[pltpu.VMEM((c + CONV_HALO, 3 * DN_WIDTH), F32),
                        pltpu.VMEM((DN_HEADS, DN_HEAD_DIM, DN_HEAD_DIM), F32)],
        compiler_params=_params("parallel", "arbitrary"),
        name="deltanet",
    )(qkv, bd, bdt, cw, alog2, dtb2, alogt, dtbt)


def _rms(x, w):
    return x * lax.rsqrt(jnp.mean(x * x, axis=-1, keepdims=True) + EPS) * w


def _post_kernel(o_ref, z_ref, ga_ref, gb_ref, ya_ref, x_ref, dnw_ref, wdn_ref, wmix_ref, fnw_ref,
                 h1_ref, xn2t_ref):
    o = o_ref[...].astype(F32)
    z = z_ref[...].astype(F32)
    parts = []
    for h in range(DN_HEADS):
        sl = slice(h * DN_HEAD_DIM, (h + 1) * DN_HEAD_DIM)
        parts.append(_rms(o[:, sl], dnw_ref[...]))
    on = jnp.concatenate(parts, axis=-1) * (z * _sigmoid(z))
    yb = jnp.dot(on.astype(BF16), wdn_ref[...], preferred_element_type=F32)
    merged = (_sigmoid(ga_ref[...].astype(F32)) * ya_ref[...].astype(F32)
              + _sigmoid(gb_ref[...].astype(F32)) * yb)
    h1 = x_ref[...] + jnp.dot(merged.astype(BF16), wmix_ref[...], preferred_element_type=F32)
    h1_ref[...] = h1
    xn2t_ref[...] = _rms(h1, fnw_ref[...]).T.astype(BF16)


def _post(o, z, ga, gb, ya, x2, dnw, wdn, wmix, fnw):
    n = x2.shape[0]
    t = POST_TILE
    row = pl.BlockSpec((t, D_MODEL), lambda i: (i, 0))
    return pl.pallas_call(
        _post_kernel,
        grid=(n // t,),
        in_specs=[row, row, row, row, row, row, _const_spec(dnw.shape), _const_spec(wdn.shape),
                  _const_spec(wmix.shape), _const_spec(fnw.shape)],
        out_specs=[row, pl.BlockSpec((D_MODEL, t), lambda i: (0, i))],
        out_shape=[jax.ShapeDtypeStruct((n, D_MODEL), F32), jax.ShapeDtypeStruct((D_MODEL, n), BF16)],
        compiler_params=_params("parallel"),
        name="post",
    )(o, z, ga, gb, ya, x2, dnw, wdn, wmix, fnw)


_MARK_BASE = -2.0 ** 120
_MARK_STEP = 1.0 / 32.0
_FAST_FLOOR = -2.0 ** 119


def _extract_fast(x, rounds, vals_ref=None):
    ranks, bad = [], []
    for l0 in range(0, x.shape[1], V7X_LANES):
        lanes = slice(l0, l0 + V7X_LANES)
        work = x[:, lanes]
        too_low = jnp.min(work, axis=0, keepdims=True) < _FAST_FLOOR
        for r in range(rounds):
            m = jnp.max(work, axis=0, keepdims=True)
            if vals_ref is not None:
                vals_ref[r:r + 1, lanes] = m
            work = jnp.where(work == m, _MARK_BASE * (1.0 + r * _MARK_STEP), work)
        hit = work <= _MARK_BASE
        ranks.append(jnp.where(hit, (work * (1.0 / _MARK_BASE) - 1.0) * (1.0 / _MARK_STEP), float(rounds)))
        miscount = jnp.abs(jnp.sum(hit.astype(F32), axis=0, keepdims=True) - float(rounds))
        bad.append(jnp.where(too_low, 1.0, miscount))
    ok = jnp.max(jnp.concatenate(bad, axis=1)) == 0.0
    return jnp.concatenate(ranks, axis=1), ok


def _extract_exact(work, rounds, vals_ref=None):
    rows = work.shape[0]
    iota = lax.broadcasted_iota(jnp.int32, work.shape, 0).astype(F32)
    rank = jnp.full(work.shape, float(rounds), F32)
    for r in range(rounds):
        m = jnp.max(work, axis=0, keepdims=True)
        idx = jnp.min(jnp.where(work == m, iota, float(rows)), axis=0, keepdims=True)
        hit = iota == idx
        rank = jnp.where(hit, float(r), rank)
        work = jnp.where(hit, -jnp.inf, work)
        if vals_ref is not None:
            vals_ref[r:r + 1, :] = m
    return rank


def _bcast_rows_bf16(row, rows):
    packed = jnp.broadcast_to(row, (BF16_SUBLANES, row.shape[1])).astype(BF16)
    return jnp.concatenate([packed] * (rows // BF16_SUBLANES), axis=0)


def _peer_kernel(xnt_ref, h1_ref, wq_ref, k1_ref, k2_ref, u_first_ref, u_odd_ref, u_even_ref,
                 vt_odd_ref, vt_even_ref, vt_last_ref, fw_ref, out_ref,
                 e1_ref, n1_ref, r2_ref, e2_ref, rk1_ref, rk2_ref, v1_ref, v2_ref, cand_ref, sel_ref, yt_ref,
                 h_even_ref, h_odd_ref, p_even_ref, p_odd_ref, q_ref):
    ib = pl.program_id(1)
    nib = pl.num_programs(1)
    t = xnt_ref.shape[1]

    @pl.when(ib == 0)
    def _():
        yt_ref[...] = jnp.zeros(yt_ref.shape, F32)
        p_odd_ref[...] = jnp.zeros(p_odd_ref.shape, BF16)
        xnt = xnt_ref[...]
        h_even_ref[...] = jnp.dot(u_first_ref[...], xnt, preferred_element_type=F32)
        q_ref[...] = jnp.dot(wq_ref[...], xnt, preferred_element_type=F32).astype(BF16)

        def head(h, carry):
            r0 = pl.multiple_of(h * 2 * PEER_HALF, 2 * PEER_HALF)
            s1 = jnp.dot(k1_ref[h], q_ref[pl.ds(r0, PEER_HALF), :], preferred_element_type=F32)
            s2 = jnp.dot(k2_ref[h], q_ref[pl.ds(r0 + PEER_HALF, PEER_HALF), :], preferred_element_type=F32)
            rank1, ok1 = _extract_fast(s1, PEER_TOPK, v1_ref)
            rank2, ok2 = _extract_fast(s2, PEER_TOPK, v2_ref)
            rk1_ref[...] = rank1
            rk2_ref[...] = rank2

            @pl.when(jnp.logical_not(ok1 & ok2))
            def _():
                rk1_ref[...] = _extract_exact(s1, PEER_TOPK, v1_ref)
                rk2_ref[...] = _extract_exact(s2, PEER_TOPK, v2_ref)

            cand_ref[...] = jnp.full(cand_ref.shape, -jnp.inf, F32)
            for n, (a, b) in enumerate(PEER_CANDS):
                cand_ref[n:n + 1, :] = v1_ref[a:a + 1, :] + v2_ref[b:b + 1, :]
            cand = cand_ref[...]
            valid = lax.broadcasted_iota(jnp.int32, cand.shape, 0) < len(PEER_CANDS)
            crank, ok3 = _extract_fast(jnp.where(valid, cand, _FAST_FLOOR), PEER_TOPK)
            sel_ref[...] = crank

            @pl.when(jnp.logical_not(ok3))
            def _():
                sel_ref[...] = _extract_exact(cand, PEER_TOPK)

            sel = sel_ref[...] < float(PEER_TOPK)
            cexp = jnp.where(sel, jnp.exp(jnp.where(sel, cand - cand[0:1, :], 0.0)), 0.0)
            inv_z = 1.0 / jnp.sum(cexp, axis=0, keepdims=True)
            self32 = sel.astype(F32)
            rank1 = rk1_ref[...].astype(BF16)
            n1 = jnp.zeros(rank1.shape, BF16)
            n = 0
            for a in range(PEER_TOPK):
                width = PEER_TOPK // (a + 1)
                n_a = jnp.sum(self32[n:n + width, :], axis=0, keepdims=True)
                n1 = jnp.where(rank1 == jnp.asarray(a, BF16), _bcast_rows_bf16(n_a, N_KEYS), n1)
                n += width
            e1_ref[h] = jnp.exp(s1 - v1_ref[0:1, :]) * (0.5 * inv_z)
            n1_ref[h] = n1.astype(F32)
            r2_ref[h] = rk2_ref[...].astype(BF16)
            e2_ref[h] = jnp.exp(s2 - v2_ref[0:1, :]).astype(BF16)
            return carry

        lax.fori_loop(0, PEER_HEADS, head, 0)

    th = t // 2

    def stage(blk, hf, u_next_ref, h_next_ref, h_cur_ref, p_cur_ref, vt_prev_ref, p_prev_ref):
        lanes = slice(hf * th, (hf + 1) * th)
        h_next_ref[:, lanes] = jnp.dot(u_next_ref[...], xnt_ref[:, lanes], preferred_element_type=F32)
        for kk in range(PEER_IBLK):
            i = blk * PEER_IBLK + kk
            rows = slice(kk * N_KEYS, (kk + 1) * N_KEYS)
            hk = h_cur_ref[rows, lanes]
            act = hk * (1.0 + lax.erf(hk * 0.7071067811865476))
            g = jnp.zeros((N_KEYS, th), BF16)
            for h in range(PEER_HEADS):
                e1row = _bcast_rows_bf16(e1_ref[h, pl.ds(i, 1), lanes], N_KEYS)
                n1row = _bcast_rows_bf16(n1_ref[h, pl.ds(i, 1), lanes], N_KEYS)
                g = g + e1row * jnp.where(r2_ref[h, :, lanes] < n1row, e2_ref[h, :, lanes], jnp.zeros((), BF16))
            p_cur_ref[rows, lanes] = g * act.astype(BF16)
        yt_ref[:, lanes] += jnp.dot(vt_prev_ref[...], p_prev_ref[:, lanes], preferred_element_type=F32)

    for hf in range(2):
        stage(2 * ib, hf, u_odd_ref, h_odd_ref, h_even_ref, p_even_ref, vt_odd_ref, p_odd_ref)
    for hf in range(2):
        stage(2 * ib + 1, hf, u_even_ref, h_even_ref, h_odd_ref, p_odd_ref, vt_even_ref, p_even_ref)

    @pl.when(ib == nib - 1)
    def _():
        yt = yt_ref[...] + jnp.dot(vt_last_ref[...], p_odd_ref[...], preferred_element_type=F32)
        hfin = h1_ref[...] + yt.T
        out_ref[...] = _rms(hfin, fw_ref[...])


def _peer(xn2t, h1, wqt, k1, k2, u, vt, fw):
    n = xn2t.shape[1]
    t = PEER_TILE
    eb = PEER_IBLK * N_KEYS
    nblk = (N_KEYS * N_KEYS) // eb
    assert nblk % 2 == 0
    tok = pl.BlockSpec((t, D_MODEL), lambda i, j: (i, 0))
    big = pltpu.VMEM((PEER_HEADS, N_KEYS, t), F32)
    bigb = pltpu.VMEM((PEER_HEADS, N_KEYS, t), BF16)
    keys = pltpu.VMEM((N_KEYS, t), F32)
    u_spec = lambda f: pl.BlockSpec((eb, D_MODEL), lambda i, j: (f(j), 0))
    vt_spec = lambda f: pl.BlockSpec((D_MODEL, eb), lambda i, j: (0, f(j)))
    return pl.pallas_call(
        _peer_kernel,
        grid=(n // t, nblk // 2),
        in_specs=[pl.BlockSpec((D_MODEL, t), lambda i, j: (0, i)), tok,
                  _const_spec(wqt.shape), _const_spec(k1.shape), _const_spec(k2.shape),
                  u_spec(lambda j: 0), u_spec(lambda j: 2 * j + 1),
                  u_spec(lambda j: jnp.minimum(2 * j + 2, nblk - 1)),
                  vt_spec(lambda j: jnp.maximum(2 * j - 1, 0)), vt_spec(lambda j: 2 * j),
                  vt_spec(lambda j: nblk - 1),
                  _const_spec(fw.shape)],
        out_specs=tok,
        out_shape=jax.ShapeDtypeStruct((n, D_MODEL), F32),
        scratch_shapes=[big, big, bigb, bigb, keys, keys,
                        pltpu.VMEM((PEER_TOPK, t), F32), pltpu.VMEM((PEER_TOPK, t), F32),
                        pltpu.VMEM((PEER_CAND_ROWS, t), F32), pltpu.VMEM((PEER_CAND_ROWS, t), F32),
                        pltpu.VMEM((D_MODEL, t), F32),
                        pltpu.VMEM((eb, t), F32), pltpu.VMEM((eb, t), F32),
                        pltpu.VMEM((eb, t), BF16), pltpu.VMEM((eb, t), BF16),
                        pltpu.VMEM((2 * PEER_HALF * PEER_HEADS, t), BF16)],
        compiler_params=_params("parallel", "arbitrary"),
        name="peer",
    )(xn2t, h1, wqt, k1, k2, u, u, u, vt, vt, vt, fw)


def kernel(x, mix_norm_w, w_in, pool_w, pool_scale, conv_w, a_log, dt_bias, dn_norm_w, w_pool_up,
           w_dn_up, w_mix_out, ffn_norm_w, peer_w_query, peer_keys_1, peer_keys_2, peer_down, peer_up,
           final_norm_w):
    b, s, d = x.shape
    n = b * s
    assert d == D_MODEL and s % POOL_TILE == 0 and s % DN_STEP == 0
    assert n % IN_TILE == 0 and n % POST_TILE == 0 and n % PEER_TILE == 0
    assert w_in.shape[0] == 1, "single-layer block"
    l = 0
    h = x.reshape(n, d)

    c0 = POOL_WIDTH
    c1 = c0 + 3 * DN_WIDTH
    c2 = c1 + DN_WIDTH
    c3 = c2 + 2 * DN_HEADS
    c4 = c3 + D_MODEL
    wi = w_in[l]
    xa, qkv, z, ga, gb, bd = _in_proj(
        h, mix_norm_w[l].reshape(1, d),
        wi[:, :c0].astype(BF16), wi[:, c0:c1].astype(BF16), wi[:, c1:c2].astype(BF16),
        wi[:, c3:c4].astype(BF16), wi[:, c4:].astype(BF16), wi[:, c2:c3].astype(BF16))

    ya = _pool(xa.reshape(b, s, POOL_WIDTH), pool_w[l].astype(BF16), pool_scale[l].reshape(1, POOL_WIDTH),
               w_pool_up[l].astype(BF16))

    bd3 = bd.reshape(b, s, 2 * DN_HEADS)
    o = _deltanet(qkv.reshape(b, s, 3 * DN_WIDTH), bd3, bd3.transpose(0, 2, 1), conv_w[l], a_log[l],
                  dt_bias[l])

    h1, xn2t = _post(o.reshape(n, DN_WIDTH), z, ga, gb, ya.reshape(n, d), h,
                    dn_norm_w[l].reshape(1, DN_HEAD_DIM), w_dn_up[l].astype(BF16),
                    w_mix_out[l].astype(BF16), ffn_norm_w[l].reshape(1, d))

    out = _peer(xn2t, h1, peer_w_query[l].T.astype(BF16), peer_keys_1[l].astype(BF16),
                peer_keys_2[l].astype(BF16), peer_down[l].astype(BF16), peer_up[l].T.astype(BF16),
                final_norm_w.reshape(1, d))
    return out.reshape(b, s, d)
```

```python
import functools

import jax
import jax.numpy as jnp
from jax import lax
from jax.experimental import pallas as pl
from jax.experimental.pallas import tpu as pltpu

F32 = jnp.float32
BF16 = jnp.bfloat16
EPS = 1e-6

D_MODEL = 1024
POOL_WINDOWS = (2, 4, 8, 16)
POOL_GROUP_DIM = 128
POOL_WIDTH = 512
POOL_HALO = 16
DN_HEADS = 8
DN_HEAD_DIM = 128
DN_WIDTH = DN_HEADS * DN_HEAD_DIM
CONV_WIDTH = 4
CONV_HALO = 8
DN_STEP = 1024
DN_CHUNK = 128
PEER_HEADS = 8
N_KEYS = 128
PEER_TOPK = 16
PEER_HALF = 128

V7X_LANES = 128
BF16_SUBLANES = 16
VMEM_LIMIT = 56 * 1024 * 1024

IN_TILE = 512
POOL_TILE = 1024
POST_TILE = 1024
PEER_TILE = 512
PEER_IBLK = 4

PEER_CANDS = tuple((a, b) for a in range(PEER_TOPK) for b in range(PEER_TOPK)
                   if (a + 1) * (b + 1) <= PEER_TOPK)
PEER_CAND_ROWS = 56

_NT = (((1,), (1,)), ((), ()))
_TN = (((0,), (0,)), ((), ()))


def _params(*sem):
    return pltpu.CompilerParams(dimension_semantics=sem, vmem_limit_bytes=VMEM_LIMIT)


def _const_spec(shape):
    return pl.BlockSpec(shape, lambda *_: (0,) * len(shape))


def _resident_spec(shape):
    return pl.BlockSpec(shape, lambda *_: (0,) * len(shape), pipeline_mode=pl.Buffered(1))


def _sigmoid(x):
    return 1.0 / (1.0 + jnp.exp(-x))


def _in_proj_kernel(x_ref, nw_ref, wxa_ref, wqkv_ref, wz_ref, wga_ref, wgb_ref, wbd_ref,
                    xa_ref, qkv_ref, z_ref, ga_ref, gb_ref, bd_ref):
    x = x_ref[...]
    xn = x * lax.rsqrt(jnp.mean(x * x, axis=-1, keepdims=True) + EPS) * nw_ref[...]
    xb = xn.astype(BF16)
    xa_ref[...] = jnp.dot(xb, wxa_ref[...], preferred_element_type=F32)
    qkv_ref[...] = jnp.dot(xb, wqkv_ref[...], preferred_element_type=F32)
    z_ref[...] = jnp.dot(xb, wz_ref[...], preferred_element_type=F32).astype(BF16)
    ga_ref[...] = jnp.dot(xb, wga_ref[...], preferred_element_type=F32).astype(BF16)
    gb_ref[...] = jnp.dot(xb, wgb_ref[...], preferred_element_type=F32).astype(BF16)
    bd_ref[...] = jnp.dot(xb, wbd_ref[...], preferred_element_type=F32)


def _in_proj(x2, nw, wxa, wqkv, wz, wga, wgb, wbd):
    n = x2.shape[0]
    t = IN_TILE
    row = lambda w: pl.BlockSpec((t, w), lambda i: (i, 0))
    widths = (POOL_WIDTH, 3 * DN_WIDTH, DN_WIDTH, D_MODEL, D_MODEL, 2 * DN_HEADS)
    dtypes = (F32, F32, BF16, BF16, BF16, F32)
    return pl.pallas_call(
        _in_proj_kernel,
        grid=(n // t,),
        in_specs=[row(D_MODEL), _const_spec(nw.shape), _resident_spec(wxa.shape), _resident_spec(wqkv.shape),
                  _resident_spec(wz.shape), _resident_spec(wga.shape), _resident_spec(wgb.shape),
                  _const_spec(wbd.shape)],
        out_specs=[row(w) for w in widths],
        out_shape=[jax.ShapeDtypeStruct((n, w), dt) for w, dt in zip(widths, dtypes)],
        compiler_params=_params("parallel"),
        name="in_proj",
    )(x2, nw, wxa, wqkv, wz, wga, wgb, wbd)


def _pool_kernel(xa_ref, pw_ref, ps_ref, wup_ref, ya_ref, buf_ref):
    s = pl.program_id(1)
    t = xa_ref.shape[1]

    @pl.when(s == 0)
    def _():
        buf_ref[0:POOL_HALO, :] = jnp.zeros((POOL_HALO, POOL_WIDTH), F32)

    @pl.when(s > 0)
    def _():
        buf_ref[0:POOL_HALO, :] = buf_ref[t:t + POOL_HALO, :]

    buf_ref[POOL_HALO:, :] = xa_ref[0]
    pos = s * t + lax.broadcasted_iota(jnp.int32, (t, 1), 0)
    ys = []
    for g, win in enumerate(POOL_WINDOWS):
        sl = slice(g * POOL_GROUP_DIM, (g + 1) * POOL_GROUP_DIM)
        xg = buf_ref[POOL_HALO:POOL_HALO + t, sl]
        acc = xg
        for k in range(1, win):
            acc = acc + buf_ref[POOL_HALO - k:POOL_HALO - k + t, sl]
        cnt = jnp.minimum(pos + 1, win).astype(F32)
        pooled = acc / cnt - xg
        y = jnp.dot(pooled.astype(BF16), pw_ref[g], preferred_element_type=F32)
        ys.append(y * ps_ref[:, sl])
    y = jnp.concatenate(ys, axis=-1)
    ya_ref[0] = jnp.dot(y.astype(BF16), wup_ref[...], preferred_element_type=F32).astype(BF16)


def _pool(xa, pw, ps, wup):
    b, s, _ = xa.shape
    t = POOL_TILE
    return pl.pallas_call(
        _pool_kernel,
        grid=(b, s // t),
        in_specs=[pl.BlockSpec((1, t, POOL_WIDTH), lambda i, j: (i, j, 0)),
                  _const_spec(pw.shape), _const_spec(ps.shape), _const_spec(wup.shape)],
        out_specs=pl.BlockSpec((1, t, D_MODEL), lambda i, j: (i, j, 0)),
        out_shape=jax.ShapeDtypeStruct((b, s, D_MODEL), BF16),
        scratch_shapes=[pltpu.VMEM((t + POOL_HALO, POOL_WIDTH), F32)],
        compiler_params=_params("parallel", "arbitrary"),
        name="pool",
    )(xa, pw, ps, wup)


def _softplus(x):
    return jnp.maximum(x, 0.0) + jnp.log1p(jnp.exp(-jnp.abs(x)))


def _dn_kernel(qkv_ref, bd_ref, bdt_ref, cw_ref, alog_ref, dtb_ref, alogt_ref, dtbt_ref,
               o_ref, cbuf_ref, st_ref):
    s = pl.program_id(1)
    c = DN_CHUNK
    hd = DN_HEAD_DIM

    @pl.when(s == 0)
    def _():
        cbuf_ref[0:CONV_HALO, :] = jnp.zeros((CONV_HALO, 3 * DN_WIDTH), F32)
        st_ref[...] = jnp.zeros(st_ref.shape, F32)

    @pl.when(s > 0)
    def _():
        cbuf_ref[0:CONV_HALO, :] = cbuf_ref[DN_STEP:DN_STEP + CONV_HALO, :]

    cbuf_ref[CONV_HALO:, :] = qkv_ref[0]
    for r0 in range(0, DN_STEP, c):
        _dn_chunk(r0, bd_ref, bdt_ref, cw_ref, alog_ref, dtb_ref, alogt_ref, dtbt_ref, o_ref, cbuf_ref, st_ref)


def _dn_chunk(r0, bd_ref, bdt_ref, cw_ref, alog_ref, dtb_ref, alogt_ref, dtbt_ref, o_ref, cbuf_ref, st_ref):
    c = DN_CHUNK
    hd = DN_HEAD_DIM
    bd = bd_ref[0, r0:r0 + c, :]
    bdt = bdt_ref[0, :, r0:r0 + c]
    beta_all = _sigmoid(bd[:, 0:DN_HEADS])
    g_all = -jnp.exp(alog_ref[...]) * _softplus(bd[:, DN_HEADS:] + dtb_ref[...])
    g_allt = -jnp.exp(alogt_ref[...]) * _softplus(bdt[DN_HEADS:, :] + dtbt_ref[...])

    row = lax.broadcasted_iota(jnp.int32, (c, c), 0)
    col = lax.broadcasted_iota(jnp.int32, (c, c), 1)
    tril = row >= col
    strict = row > col
    gc = jnp.dot(tril.astype(F32), g_all, preferred_element_type=F32, precision=lax.Precision.HIGHEST)
    gct = jnp.dot(g_allt, (row <= col).astype(F32), preferred_element_type=F32,
                  precision=lax.Precision.HIGHEST)
    eye = (row == col).astype(F32)

    def conv_silu(lane0):
        sl = slice(lane0, lane0 + hd)
        y = cw_ref[0:1, sl] * cbuf_ref[r0 + CONV_HALO - 3:r0 + CONV_HALO - 3 + c, sl]
        for j in range(1, CONV_WIDTH):
            y = y + cw_ref[j:j + 1, sl] * cbuf_ref[r0 + CONV_HALO - 3 + j:r0 + CONV_HALO - 3 + j + c, sl]
        return y * _sigmoid(y)

    def l2n(v):
        return v * lax.rsqrt(jnp.sum(v * v, axis=-1, keepdims=True) + EPS)

    hs = range(DN_HEADS)
    q = [l2n(conv_silu(h * hd)) * (hd ** -0.5) for h in hs]
    k = [l2n(conv_silu(DN_WIDTH + h * hd)) for h in hs]
    v = [conv_silu(2 * DN_WIDTH + h * hd) for h in hs]
    gcol = [gc[:, h:h + 1] for h in hs]
    grow = [gct[h:h + 1, :] for h in hs]
    decay = [jnp.where(tril, jnp.exp(jnp.where(tril, gcol[h] - grow[h], 0.0)), 0.0) for h in hs]
    kb = [k[h] * beta_all[:, h:h + 1] for h in hs]
    vb = [v[h] * beta_all[:, h:h + 1] for h in hs]
    kf = [k[h].astype(BF16) for h in hs]
    lower = [jnp.where(strict, lax.dot_general(kb[h].astype(BF16), kf[h], _NT, preferred_element_type=F32)
                       * decay[h], 0.0) for h in hs]
    attn = [(lax.dot_general(q[h].astype(BF16), kf[h], _NT, preferred_element_type=F32) * decay[h]).astype(BF16)
            for h in hs]
    pair = ((row ^ col) == 1) & strict
    tinv = [eye - jnp.where(pair, lower[h], 0.0) for h in hs]
    blk = 2
    while blk < c:
        m = ((row ^ col) < 2 * blk) & ((row & blk) != 0) & ((col & blk) == 0)
        tb = [tinv[h].astype(BF16) for h in hs]
        left = [jnp.dot(tb[h], jnp.where(m, lower[h], 0.0).astype(BF16), preferred_element_type=F32).astype(BF16)
                for h in hs]
        tinv = [tinv[h] - jnp.dot(left[h], tb[h], preferred_element_type=F32) for h in hs]
        blk *= 2
    eg = [jnp.exp(gcol[h]) for h in hs]
    uw = [jnp.dot(tinv[h].astype(BF16),
                  jnp.concatenate([vb[h], kb[h] * eg[h]], axis=-1).astype(BF16), preferred_element_type=F32)
          for h in hs]
    st = [st_ref[h] for h in hs]
    ws = [jnp.dot(jnp.concatenate([uw[h][:, hd:], q[h] * eg[h]], axis=0).astype(BF16), st[h].astype(BF16),
                  preferred_element_type=F32) for h in hs]
    vnb = [(uw[h][:, :hd] - ws[h][:c]).astype(BF16) for h in hs]
    o = [ws[h][c:] + jnp.dot(attn[h], vnb[h], preferred_element_type=F32) for h in hs]
    for h in hs:
        g_last = grow[h][:, c - 1:c]
        k_dec = (k[h] * jnp.exp(g_last - gcol[h])).astype(BF16)
        st_ref[h] = st[h] * jnp.exp(g_last) + lax.dot_general(k_dec, vnb[h], _TN, preferred_element_type=F32)
    o_ref[0, r0:r0 + c, :] = jnp.concatenate(o, axis=-1).astype(BF16)


def _deltanet(qkv, bd, bdt, cw, alog, dtb):
    b, s, _ = qkv.shape
    c = DN_STEP
    alog2, dtb2 = alog.reshape(1, DN_HEADS), dtb.reshape(1, DN_HEADS)
    alogt, dtbt = alog.reshape(DN_HEADS, 1), dtb.reshape(DN_HEADS, 1)
    return pl.pallas_call(
        _dn_kernel,
        grid=(b, s // c),
        in_specs=[pl.BlockSpec((1, c, 3 * DN_WIDTH), lambda i, j: (i, j, 0)),
                  pl.BlockSpec((1, c, 2 * DN_HEADS), lambda i, j: (i, j, 0)),
                  pl.BlockSpec((1, 2 * DN_HEADS, c), lambda i, j: (i, 0, j)),
                  _const_spec(cw.shape), _const_spec(alog2.shape), _const_spec(dtb2.shape),
                  _const_spec(alogt.shape), _const_spec(dtbt.shape)],
        out_specs=pl.BlockSpec((1, c, DN_WIDTH), lambda i, j: (i, j, 0)),
        out_shape=jax.ShapeDtypeStruct((b, s, DN_WIDTH), BF16),
        scratch_shapes=[pltpu.VMEM((c + CONV_HALO, 3 * DN_WIDTH), F32),
                        pltpu.VMEM((DN_HEADS, DN_HEAD_DIM, DN_HEAD_DIM), F32)],
        compiler_params=_params("parallel", "arbitrary"),
        name="deltanet",
    )(qkv, bd, bdt, cw, alog2, dtb2, alogt, dtbt)


def _rms(x, w):
    return x * lax.rsqrt(jnp.mean(x * x, axis=-1, keepdims=True) + EPS) * w


def _post_kernel(o_ref, z_ref, ga_ref, gb_ref, ya_ref, x_ref, dnw_ref, wdn_ref, wmix_ref, fnw_ref,
                 h1_ref, xn2t_ref):
    o = o_ref[...].astype(F32)
    z = z_ref[...].astype(F32)
    parts = []
    for h in range(DN_HEADS):
        sl = slice(h * DN_HEAD_DIM, (h + 1) * DN_HEAD_DIM)
        parts.append(_rms(o[:, sl], dnw_ref[...]))
    on = jnp.concatenate(parts, axis=-1) * (z * _sigmoid(z))
    yb = jnp.dot(on.astype(BF16), wdn_ref[...], preferred_element_type=F32)
    merged = (_sigmoid(ga_ref[...].astype(F32)) * ya_ref[...].astype(F32)
              + _sigmoid(gb_ref[...].astype(F32)) * yb)
    h1 = x_ref[...] + jnp.dot(merged.astype(BF16), wmix_ref[...], preferred_element_type=F32)
    h1_ref[...] = h1
    xn2t_ref[...] = _rms(h1, fnw_ref[...]).T.astype(BF16)


def _post(o, z, ga, gb, ya, x2, dnw, wdn, wmix, fnw):
    n = x2.shape[0]
    t = POST_TILE
    row = pl.BlockSpec((t, D_MODEL), lambda i: (i, 0))
    return pl.pallas_call(
        _post_kernel,
        grid=(n // t,),
        in_specs=[row, row, row, row, row, row, _const_spec(dnw.shape), _const_spec(wdn.shape),
                  _const_spec(wmix.shape), _const_spec(fnw.shape)],
        out_specs=[row, pl.BlockSpec((D_MODEL, t), lambda i: (0, i))],
        out_shape=[jax.ShapeDtypeStruct((n, D_MODEL), F32), jax.ShapeDtypeStruct((D_MODEL, n), BF16)],
        compiler_params=_params("parallel"),
        name="post",
    )(o, z, ga, gb, ya, x2, dnw, wdn, wmix, fnw)


_MARK_BASE = -2.0 ** 120
_MARK_STEP = 1.0 / 32.0
_FAST_FLOOR = -2.0 ** 119


def _extract_fast(x, rounds, vals_ref=None):
    ranks, bad = [], []
    for l0 in range(0, x.shape[1], V7X_LANES):
        lanes = slice(l0, l0 + V7X_LANES)
        work = x[:, lanes]
        too_low = jnp.min(work, axis=0, keepdims=True) < _FAST_FLOOR
        for r in range(rounds):
            m = jnp.max(work, axis=0, keepdims=True)
            if vals_ref is not None:
                vals_ref[r:r + 1, lanes] = m
            work = jnp.where(work == m, _MARK_BASE * (1.0 + r * _MARK_STEP), work)
        hit = work <= _MARK_BASE
        ranks.append(jnp.where(hit, (work * (1.0 / _MARK_BASE) - 1.0) * (1.0 / _MARK_STEP), float(rounds)))
        miscount = jnp.abs(jnp.sum(hit.astype(F32), axis=0, keepdims=True) - float(rounds))
        bad.append(jnp.where(too_low, 1.0, miscount))
    ok = jnp.max(jnp.concatenate(bad, axis=1)) == 0.0
    return jnp.concatenate(ranks, axis=1), ok


def _extract_exact(work, rounds, vals_ref=None):
    rows = work.shape[0]
    iota = lax.broadcasted_iota(jnp.int32, work.shape, 0).astype(F32)
    rank = jnp.full(work.shape, float(rounds), F32)
    for r in range(rounds):
        m = jnp.max(work, axis=0, keepdims=True)
        idx = jnp.min(jnp.where(work == m, iota, float(rows)), axis=0, keepdims=True)
        hit = iota == idx
        rank = jnp.where(hit, float(r), rank)
        work = jnp.where(hit, -jnp.inf, work)
        if vals_ref is not None:
            vals_ref[r:r + 1, :] = m
    return rank


def _bcast_rows_bf16(row, rows):
    packed = jnp.broadcast_to(row, (BF16_SUBLANES, row.shape[1])).astype(BF16)
    return jnp.concatenate([packed] * (rows // BF16_SUBLANES), axis=0)


def _peer_kernel(xnt_ref, h1_ref, wq_ref, k1_ref, k2_ref, u_first_ref, u_odd_ref, u_even_ref,
                 vt_odd_ref, vt_even_ref, vt_last_ref, fw_ref, out_ref,
                 e1_ref, n1_ref, r2_ref, e2_ref, rk1_ref, rk2_ref, v1_ref, v2_ref, cand_ref, sel_ref, yt_ref,
                 h_even_ref, h_odd_ref, p_even_ref, p_odd_ref, q_ref):
    ib = pl.program_id(1)
    nib = pl.num_programs(1)
    t = xnt_ref.shape[1]

    @pl.when(ib == 0)
    def _():
        yt_ref[...] = jnp.zeros(yt_ref.shape, F32)
        p_odd_ref[...] = jnp.zeros(p_odd_ref.shape, BF16)
        xnt = xnt_ref[...]
        h_even_ref[...] = jnp.dot(u_first_ref[...], xnt, preferred_element_type=F32)
        q_ref[...] = jnp.dot(wq_ref[...], xnt, preferred_element_type=F32).astype(BF16)

        def head(h, carry):
            r0 = pl.multiple_of(h * 2 * PEER_HALF, 2 * PEER_HALF)
            s1 = jnp.dot(k1_ref[h], q_ref[pl.ds(r0, PEER_HALF), :], preferred_element_type=F32)
            s2 = jnp.dot(k2_ref[h], q_ref[pl.ds(r0 + PEER_HALF, PEER_HALF), :], preferred_element_type=F32)
            rank1, ok1 = _extract_fast(s1, PEER_TOPK, v1_ref)
            rank2, ok2 = _extract_fast(s2, PEER_TOPK, v2_ref)
            rk1_ref[...] = rank1
            rk2_ref[...] = rank2

            @pl.when(jnp.logical_not(ok1 & ok2))
            def _():
                rk1_ref[...] = _extract_exact(s1, PEER_TOPK, v1_ref)
                rk2_ref[...] = _extract_exact(s2, PEER_TOPK, v2_ref)

            cand_ref[...] = jnp.full(cand_ref.shape, -jnp.inf, F32)
            for n, (a, b) in enumerate(PEER_CANDS):
                cand_ref[n:n + 1, :] = v1_ref[a:a + 1, :] + v2_ref[b:b + 1, :]
            cand = cand_ref[...]
            valid = lax.broadcasted_iota(jnp.int32, cand.shape, 0) < len(PEER_CANDS)
            crank, ok3 = _extract_fast(jnp.where(valid, cand, _FAST_FLOOR), PEER_TOPK)
            sel_ref[...] = crank

            @pl.when(jnp.logical_not(ok3))
            def _():
                sel_ref[...] = _extract_exact(cand, PEER_TOPK)

            sel = sel_ref[...] < float(PEER_TOPK)
            cexp = jnp.where(sel, jnp.exp(jnp.where(sel, cand - cand[0:1, :], 0.0)), 0.0)
            inv_z = 1.0 / jnp.sum(cexp, axis=0, keepdims=True)
            self32 = sel.astype(F32)
            rank1 = rk1_ref[...].astype(BF16)
            n1 = jnp.zeros(rank1.shape, BF16)
            n = 0
            for a in range(PEER_TOPK):
                width = PEER_TOPK // (a + 1)
                n_a = jnp.sum(self32[n:n + width, :], axis=0, keepdims=True)
                n1 = jnp.where(rank1 == jnp.asarray(a, BF16), _bcast_rows_bf16(n_a, N_KEYS), n1)
                n += width
            e1_ref[h] = jnp.exp(s1 - v1_ref[0:1, :]) * (0.5 * inv_z)
            n1_ref[h] = n1.astype(F32)
            r2_ref[h] = rk2_ref[...].astype(BF16)
            e2_ref[h] = jnp.exp(s2 - v2_ref[0:1, :]).astype(BF16)
            return carry

        lax.fori_loop(0, PEER_HEADS, head, 0)

    th = t // 2

    def stage(blk, hf, u_next_ref, h_next_ref, h_cur_ref, p_cur_ref, vt_prev_ref, p_prev_ref):
        lanes = slice(hf * th, (hf + 1) * th)
        h_next_ref[:, lanes] = jnp.dot(u_next_ref[...], xnt_ref[:, lanes], preferred_element_type=F32)
        for kk in range(PEER_IBLK):
            i = blk * PEER_IBLK + kk
            rows = slice(kk * N_KEYS, (kk + 1) * N_KEYS)
            hk = h_cur_ref[rows, lanes]
            act = hk * (1.0 + lax.erf(hk * 0.7071067811865476))
            g = jnp.zeros((N_KEYS, th), BF16)
            for h in range(PEER_HEADS):
                e1row = _bcast_rows_bf16(e1_ref[h, pl.ds(i, 1), lanes], N_KEYS)
                n1row = _bcast_rows_bf16(n1_ref[h, pl.ds(i, 1), lanes], N_KEYS)
                g = g + e1row * jnp.where(r2_ref[h, :, lanes] < n1row, e2_ref[h, :, lanes], jnp.zeros((), BF16))
            p_cur_ref[rows, lanes] = g * act.astype(BF16)
        yt_ref[:, lanes] += jnp.dot(vt_prev_ref[...], p_prev_ref[:, lanes], preferred_element_type=F32)

    for hf in range(2):
        stage(2 * ib, hf, u_odd_ref, h_odd_ref, h_even_ref, p_even_ref, vt_odd_ref, p_odd_ref)
    for hf in range(2):
        stage(2 * ib + 1, hf, u_even_ref, h_even_ref, h_odd_ref, p_odd_ref, vt_even_ref, p_even_ref)

    @pl.when(ib == nib - 1)
    def _():
        yt = yt_ref[...] + jnp.dot(vt_last_ref[...], p_odd_ref[...], preferred_element_type=F32)
        hfin = h1_ref[...] + yt.T
        out_ref[...] = _rms(hfin, fw_ref[...])


def _peer(xn2t, h1, wqt, k1, k2, u, vt, fw):
    n = xn2t.shape[1]
    t = PEER_TILE
    eb = PEER_IBLK * N_KEYS
    nblk = (N_KEYS * N_KEYS) // eb
    assert nblk % 2 == 0
    tok = pl.BlockSpec((t, D_MODEL), lambda i, j: (i, 0))
    big = pltpu.VMEM((PEER_HEADS, N_KEYS, t), F32)
    bigb = pltpu.VMEM((PEER_HEADS, N_KEYS, t), BF16)
    keys = pltpu.VMEM((N_KEYS, t), F32)
    u_spec = lambda f: pl.BlockSpec((eb, D_MODEL), lambda i, j: (f(j), 0))
    vt_spec = lambda f: pl.BlockSpec((D_MODEL, eb), lambda i, j: (0, f(j)))
    return pl.pallas_call(
        _peer_kernel,
        grid=(n // t, nblk // 2),
        in_specs=[pl.BlockSpec((D_MODEL, t), lambda i, j: (0, i)), tok,
                  _const_spec(wqt.shape), _const_spec(k1.shape), _const_spec(k2.shape),
                  u_spec(lambda j: 0), u_spec(lambda j: 2 * j + 1),
                  u_spec(lambda j: jnp.minimum(2 * j + 2, nblk - 1)),
                  vt_spec(lambda j: jnp.maximum(2 * j - 1, 0)), vt_spec(lambda j: 2 * j),
                  vt_spec(lambda j: nblk - 1),
                  _const_spec(fw.shape)],
        out_specs=tok,
        out_shape=jax.ShapeDtypeStruct((n, D_MODEL), F32),
        scratch_shapes=[big, big, bigb, bigb, keys, keys,
                        pltpu.VMEM((PEER_TOPK, t), F32), pltpu.VMEM((PEER_TOPK, t), F32),
                        pltpu.VMEM((PEER_CAND_ROWS, t), F32), pltpu.VMEM((PEER_CAND_ROWS, t), F32),
                        pltpu.VMEM((D_MODEL, t), F32),
                        pltpu.VMEM((eb, t), F32), pltpu.VMEM((eb, t), F32),
                        pltpu.VMEM((eb, t), BF16), pltpu.VMEM((eb, t), BF16),
                        pltpu.VMEM((2 * PEER_HALF * PEER_HEADS, t), BF16)],
        compiler_params=_params("parallel", "arbitrary"),
        name="peer",
    )(xn2t, h1, wqt, k1, k2, u, u, u, vt, vt, vt, fw)


def kernel(x, mix_norm_w, w_in, pool_w, pool_scale, conv_w, a_log, dt_bias, dn_norm_w, w_pool_up,
           w_dn_up, w_mix_out, ffn_norm_w, peer_w_query, peer_keys_1, peer_keys_2, peer_down, peer_up,
           final_norm_w):
    b, s, d = x.shape
    n = b * s
    assert d == D_MODEL and s % POOL_TILE == 0 and s % DN_STEP == 0
    assert n % IN_TILE == 0 and n % POST_TILE == 0 and n % PEER_TILE == 0
    assert w_in.shape[0] == 1, "single-layer block"
    l = 0
    h = x.reshape(n, d)

    c0 = POOL_WIDTH
    c1 = c0 + 3 * DN_WIDTH
    c2 = c1 + DN_WIDTH
    c3 = c2 + 2 * DN_HEADS
    c4 = c3 + D_MODEL
    wi = w_in[l]
    xa, qkv, z, ga, gb, bd = _in_proj(
        h, mix_norm_w[l].reshape(1, d),
        wi[:, :c0].astype(BF16), wi[:, c0:c1].astype(BF16), wi[:, c1:c2].astype(BF16),
        wi[:, c3:c4].astype(BF16), wi[:, c4:].astype(BF16), wi[:, c2:c3].astype(BF16))

    ya = _pool(xa.reshape(b, s, POOL_WIDTH), pool_w[l].astype(BF16), pool_scale[l].reshape(1, POOL_WIDTH),
               w_pool_up[l].astype(BF16))

    bd3 = bd.reshape(b, s, 2 * DN_HEADS)
    o = _deltanet(qkv.reshape(b, s, 3 * DN_WIDTH), bd3, bd3.transpose(0, 2, 1), conv_w[l], a_log[l],
                  dt_bias[l])

    h1, xn2t = _post(o.reshape(n, DN_WIDTH), z, ga, gb, ya.reshape(n, d), h,
                    dn_norm_w[l].reshape(1, DN_HEAD_DIM), w_dn_up[l].astype(BF16),
                    w_mix_out[l].astype(BF16), ffn_norm_w[l].reshape(1, d))

    out = _peer(xn2t, h1, peer_w_query[l].T.astype(BF16), peer_keys_1[l].astype(BF16),
                peer_keys_2[l].astype(BF16), peer_down[l].astype(BF16), peer_up[l].T.astype(BF16),
                final_norm_w.reshape(1, d))
    return out.reshape(b, s, d)
```

```python
import jax
import jax.numpy as jnp
from jax import lax
from jax.experimental import pallas as pl
from jax.experimental.pallas import tpu as pltpu

F32 = jnp.float32
BF16 = jnp.bfloat16
EPS = 1e-6

D_MODEL = 1024
POOL_WINDOWS = (2, 4, 8, 16)
POOL_GROUP_DIM = 128
POOL_WIDTH = 512
POOL_HALO = 16
DN_HEADS = 8
DN_HEAD_DIM = 128
DN_WIDTH = DN_HEADS * DN_HEAD_DIM
CONV_WIDTH = 4
CONV_HALO = 8
DN_STEP = 1024
DN_CHUNK = 128
PEER_HEADS = 8
N_KEYS = 128
PEER_TOPK = 16
PEER_HALF = 128

V7X_LANES = 128
BF16_SUBLANES = 16
VMEM_LIMIT = 56 * 1024 * 1024

IN_TILE = 512
POOL_TILE = 1024
POST_TILE = 1024
PEER_TILE = 512
PEER_IBLK = 4

PEER_CANDS = tuple((a, b) for a in range(PEER_TOPK) for b in range(PEER_TOPK)
                   if (a + 1) * (b + 1) <= PEER_TOPK)
PEER_CAND_ROWS = 56

_NT = (((1,), (1,)), ((), ()))
_TN = (((0,), (0,)), ((), ()))


def _params(*sem):
    return pltpu.CompilerParams(dimension_semantics=sem, vmem_limit_bytes=VMEM_LIMIT)


def _const_spec(shape):
    return pl.BlockSpec(shape, lambda *_: (0,) * len(shape))


def _resident_spec(shape):
    return pl.BlockSpec(shape, lambda *_: (0,) * len(shape), pipeline_mode=pl.Buffered(1))


def _sigmoid(x):
    return 1.0 / (1.0 + jnp.exp(-x))


def _in_proj_kernel(x_ref, nw_ref, wxa_ref, wqkv_ref, wz_ref, wga_ref, wgb_ref, wbd_ref,
                    xa_ref, qkv_ref, z_ref, ga_ref, gb_ref, bd_ref):
    x = x_ref[...]
    xn = x * lax.rsqrt(jnp.mean(x * x, axis=-1, keepdims=True) + EPS) * nw_ref[...]
    xb = xn.astype(BF16)
    xa_ref[...] = jnp.dot(xb, wxa_ref[...], preferred_element_type=F32)
    qkv_ref[...] = jnp.dot(xb, wqkv_ref[...], preferred_element_type=F32)
    z_ref[...] = jnp.dot(xb, wz_ref[...], preferred_element_type=F32).astype(BF16)
    ga_ref[...] = jnp.dot(xb, wga_ref[...], preferred_element_type=F32).astype(BF16)
    gb_ref[...] = jnp.dot(xb, wgb_ref[...], preferred_element_type=F32).astype(BF16)
    bd_ref[...] = jnp.dot(xb, wbd_ref[...], preferred_element_type=F32)


def _in_proj(x2, nw, wxa, wqkv, wz, wga, wgb, wbd):
    n = x2.shape[0]
    t = IN_TILE
    row = lambda w: pl.BlockSpec((t, w), lambda i: (i, 0))
    widths = (POOL_WIDTH, 3 * DN_WIDTH, DN_WIDTH, D_MODEL, D_MODEL, 2 * DN_HEADS)
    dtypes = (F32, F32, BF16, BF16, BF16, F32)
    return pl.pallas_call(
        _in_proj_kernel,
        grid=(n // t,),
        in_specs=[row(D_MODEL), _const_spec(nw.shape), _resident_spec(wxa.shape), _resident_spec(wqkv.shape),
                  _resident_spec(wz.shape), _resident_spec(wga.shape), _resident_spec(wgb.shape),
                  _const_spec(wbd.shape)],
        out_specs=[row(w) for w in widths],
        out_shape=[jax.ShapeDtypeStruct((n, w), dt) for w, dt in zip(widths, dtypes)],
        compiler_params=_params("parallel"),
        name="in_proj",
    )(x2, nw, wxa, wqkv, wz, wga, wgb, wbd)


def _pool_kernel(xa_ref, pw_ref, ps_ref, wup_ref, ya_ref, buf_ref):
    s = pl.program_id(1)
    t = xa_ref.shape[1]

    @pl.when(s == 0)
    def _():
        buf_ref[0:POOL_HALO, :] = jnp.zeros((POOL_HALO, POOL_WIDTH), F32)

    @pl.when(s > 0)
    def _():
        buf_ref[0:POOL_HALO, :] = buf_ref[t:t + POOL_HALO, :]

    buf_ref[POOL_HALO:, :] = xa_ref[0]
    pos = s * t + lax.broadcasted_iota(jnp.int32, (t, 1), 0)
    ys = []
    for g, win in enumerate(POOL_WINDOWS):
        sl = slice(g * POOL_GROUP_DIM, (g + 1) * POOL_GROUP_DIM)
        xg = buf_ref[POOL_HALO:POOL_HALO + t, sl]
        acc = xg
        for k in range(1, win):
            acc = acc + buf_ref[POOL_HALO - k:POOL_HALO - k + t, sl]
        cnt = jnp.minimum(pos + 1, win).astype(F32)
        pooled = acc / cnt - xg
        y = jnp.dot(pooled.astype(BF16), pw_ref[g], preferred_element_type=F32)
        ys.append(y * ps_ref[:, sl])
    y = jnp.concatenate(ys, axis=-1)
    ya_ref[0] = jnp.dot(y.astype(BF16), wup_ref[...], preferred_element_type=F32).astype(BF16)


def _pool(xa, pw, ps, wup):
    b, s, _ = xa.shape
    t = POOL_TILE
    return pl.pallas_call(
        _pool_kernel,
        grid=(b, s // t),
        in_specs=[pl.BlockSpec((1, t, POOL_WIDTH), lambda i, j: (i, j, 0)),
                  _const_spec(pw.shape), _const_spec(ps.shape), _const_spec(wup.shape)],
        out_specs=pl.BlockSpec((1, t, D_MODEL), lambda i, j: (i, j, 0)),
        out_shape=jax.ShapeDtypeStruct((b, s, D_MODEL), BF16),
        scratch_shapes=[pltpu.VMEM((t + POOL_HALO, POOL_WIDTH), F32)],
        compiler_params=_params("parallel", "arbitrary"),
        name="pool",
    )(xa, pw, ps, wup)


def _softplus(x):
    return jnp.maximum(x, 0.0) + jnp.log1p(jnp.exp(-jnp.abs(x)))


def _dn_kernel(qkv_ref, bd_ref, bdt_ref, cw_ref, alog_ref, dtb_ref, alogt_ref, dtbt_ref,
               o_ref, cbuf_ref, st_ref):
    s = pl.program_id(1)
    c = DN_CHUNK
    hd = DN_HEAD_DIM

    @pl.when(s == 0)
    def _():
        cbuf_ref[0:CONV_HALO, :] = jnp.zeros((CONV_HALO, 3 * DN_WIDTH), F32)
        st_ref[...] = jnp.zeros(st_ref.shape, F32)

    @pl.when(s > 0)
    def _():
        cbuf_ref[0:CONV_HALO, :] = cbuf_ref[DN_STEP:DN_STEP + CONV_HALO, :]

    cbuf_ref[CONV_HALO:, :] = qkv_ref[0]
    for r0 in range(0, DN_STEP, c):
        _dn_chunk(r0, bd_ref, bdt_ref, cw_ref, alog_ref, dtb_ref, alogt_ref, dtbt_ref, o_ref, cbuf_ref, st_ref)


def _dn_chunk(r0, bd_ref, bdt_ref, cw_ref, alog_ref, dtb_ref, alogt_ref, dtbt_ref, o_ref, cbuf_ref, st_ref):
    c = DN_CHUNK
    hd = DN_HEAD_DIM
    bd = bd_ref[0, r0:r0 + c, :]
    bdt = bdt_ref[0, :, r0:r0 + c]
    beta_all = _sigmoid(bd[:, 0:DN_HEADS])
    g_all = -jnp.exp(alog_ref[...]) * _softplus(bd[:, DN_HEADS:] + dtb_ref[...])
    g_allt = -jnp.exp(alogt_ref[...]) * _softplus(bdt[DN_HEADS:, :] + dtbt_ref[...])

    row = lax.broadcasted_iota(jnp.int32, (c, c), 0)
    col = lax.broadcasted_iota(jnp.int32, (c, c), 1)
    tril = row >= col
    strict = row > col
    gc = jnp.dot(tril.astype(F32), g_all, preferred_element_type=F32, precision=lax.Precision.HIGHEST)
    gct = jnp.dot(g_allt, (row <= col).astype(F32), preferred_element_type=F32,
                  precision=lax.Precision.HIGHEST)
    eye = (row == col).astype(F32)

    def conv_silu(lane0):
        sl = slice(lane0, lane0 + hd)
        y = cw_ref[0:1, sl] * cbuf_ref[r0 + CONV_HALO - 3:r0 + CONV_HALO - 3 + c, sl]
        for j in range(1, CONV_WIDTH):
            y = y + cw_ref[j:j + 1, sl] * cbuf_ref[r0 + CONV_HALO - 3 + j:r0 + CONV_HALO - 3 + j + c, sl]
        return y * _sigmoid(y)

    def l2n(v):
        return v * lax.rsqrt(jnp.sum(v * v, axis=-1, keepdims=True) + EPS)

    hs = range(DN_HEADS)
    q = [l2n(conv_silu(h * hd)) * (hd ** -0.5) for h in hs]
    k = [l2n(conv_silu(DN_WIDTH + h * hd)) for h in hs]
    v = [conv_silu(2 * DN_WIDTH + h * hd) for h in hs]
    gcol = [gc[:, h:h + 1] for h in hs]
    grow = [gct[h:h + 1, :] for h in hs]
    decay = [jnp.where(tril, jnp.exp(jnp.where(tril, gcol[h] - grow[h], 0.0)), 0.0) for h in hs]
    kb = [k[h] * beta_all[:, h:h + 1] for h in hs]
    vb = [v[h] * beta_all[:, h:h + 1] for h in hs]
    kf = [k[h].astype(BF16) for h in hs]
    lower = [jnp.where(strict, lax.dot_general(kb[h].astype(BF16), kf[h], _NT, preferred_element_type=F32)
                       * decay[h], 0.0) for h in hs]
    attn = [(lax.dot_general(q[h].astype(BF16), kf[h], _NT, preferred_element_type=F32) * decay[h]).astype(BF16)
            for h in hs]
    pair = ((row ^ col) == 1) & strict
    tinv = [eye - jnp.where(pair, lower[h], 0.0) for h in hs]
    blk = 2
    while blk < c:
        m = ((row ^ col) < 2 * blk) & ((row & blk) != 0) & ((col & blk) == 0)
        tb = [tinv[h].astype(BF16) for h in hs]
        left = [jnp.dot(tb[h], jnp.where(m, lower[h], 0.0).astype(BF16), preferred_element_type=F32).astype(BF16)
                for h in hs]
        tinv = [tinv[h] - jnp.dot(left[h], tb[h], preferred_element_type=F32) for h in hs]
        blk *= 2
    eg = [jnp.exp(gcol[h]) for h in hs]
    uw = [jnp.dot(tinv[h].astype(BF16),
                  jnp.concatenate([vb[h], kb[h] * eg[h]], axis=-1).astype(BF16), preferred_element_type=F32)
          for h in hs]
    st = [st_ref[h] for h in hs]
    ws = [jnp.dot(jnp.concatenate([uw[h][:, hd:], q[h] * eg[h]], axis=0).astype(BF16), st[h].astype(BF16),
                  preferred_element_type=F32) for h in hs]
    vnb = [(uw[h][:, :hd] - ws[h][:c]).astype(BF16) for h in hs]
    o = [ws[h][c:] + jnp.dot(attn[h], vnb[h], preferred_element_type=F32) for h in hs]
    for h in hs:
        g_last = grow[h][:, c - 1:c]
        k_dec = (k[h] * jnp.exp(g_last - gcol[h])).astype(BF16)
        st_ref[h] = st[h] * jnp.exp(g_last) + lax.dot_general(k_dec, vnb[h], _TN, preferred_element_type=F32)
    o_ref[0, r0:r0 + c, :] = jnp.concatenate(o, axis=-1).astype(BF16)


def _deltanet(qkv, bd, bdt, cw, alog, dtb):
    b, s, _ = qkv.shape
    c = DN_STEP
    alog2, dtb2 = alog.reshape(1, DN_HEADS), dtb.reshape(1, DN_HEADS)
    alogt, dtbt = alog.reshape(DN_HEADS, 1), dtb.reshape(DN_HEADS, 1)
    return pl.pallas_call(
        _dn_kernel,
        grid=(b, s // c),
        in_specs=[pl.BlockSpec((1, c, 3 * DN_WIDTH), lambda i, j: (i, j, 0)),
                  pl.BlockSpec((1, c, 2 * DN_HEADS), lambda i, j: (i, j, 0)),
                  pl.BlockSpec((1, 2 * DN_HEADS, c), lambda i, j: (i, 0, j)),
                  _const_spec(cw.shape), _const_spec(alog2.shape), _const_spec(dtb2.shape),
                  _const_spec(alogt.shape), _const_spec(dtbt.shape)],
        out_specs=pl.BlockSpec((1, c, DN_WIDTH), lambda i, j: (i, j, 0)),
        out_shape=jax.ShapeDtypeStruct((b, s, DN_WIDTH), BF16),
        scratch_shapes=[pltpu.VMEM((c + CONV_HALO, 3 * DN_WIDTH), F32),
                        pltpu.VMEM((DN_HEADS, DN_HEAD_DIM, DN_HEAD_DIM), F32)],
        compiler_params=_params("parallel", "arbitrary"),
        name="deltanet",
    )(qkv, bd, bdt, cw, alog2, dtb2, alogt, dtbt)


def _rms(x, w):
    return x * lax.rsqrt(jnp.mean(x * x, axis=-1, keepdims=True) + EPS) * w


def _post_kernel(o_ref, z_ref, ga_ref, gb_ref, ya_ref, x_ref, dnw_ref, wdn_ref, wmix_ref, fnw_ref,
                 h1_ref, xn2t_ref):
    o = o_ref[...].astype(F32)
    z = z_ref[...].astype(F32)
    parts = []
    for h in range(DN_HEADS):
        sl = slice(h * DN_HEAD_DIM, (h + 1) * DN_HEAD_DIM)
        parts.append(_rms(o[:, sl], dnw_ref[...]))
    on = jnp.concatenate(parts, axis=-1) * (z * _sigmoid(z))
    yb = jnp.dot(on.astype(BF16), wdn_ref[...], preferred_element_type=F32)
    merged = (_sigmoid(ga_ref[...].astype(F32)) * ya_ref[...].astype(F32)
              + _sigmoid(gb_ref[...].astype(F32)) * yb)
    h1 = x_ref[...] + jnp.dot(merged.astype(BF16), wmix_ref[...], preferred_element_type=F32)
    h1_ref[...] = h1
    xn2t_ref[...] = _rms(h1, fnw_ref[...]).T.astype(BF16)


def _post(o, z, ga, gb, ya, x2, dnw, wdn, wmix, fnw):
    n = x2.shape[0]
    t = POST_TILE
    row = pl.BlockSpec((t, D_MODEL), lambda i: (i, 0))
    return pl.pallas_call(
        _post_kernel,
        grid=(n // t,),
        in_specs=[row, row, row, row, row, row, _const_spec(dnw.shape), _const_spec(wdn.shape),
                  _const_spec(wmix.shape), _const_spec(fnw.shape)],
        out_specs=[row, pl.BlockSpec((D_MODEL, t), lambda i: (0, i))],
        out_shape=[jax.ShapeDtypeStruct((n, D_MODEL), F32), jax.ShapeDtypeStruct((D_MODEL, n), BF16)],
        compiler_params=_params("parallel"),
        name="post",
    )(o, z, ga, gb, ya, x2, dnw, wdn, wmix, fnw)


_MARK_BASE = -2.0 ** 120
_MARK_STEP = 1.0 / 32.0
_FAST_FLOOR = -2.0 ** 119


def _extract_fast(x, rounds, vals_ref=None):
    ranks, bad = [], []
    for l0 in range(0, x.shape[1], V7X_LANES):
        lanes = slice(l0, l0 + V7X_LANES)
        work = x[:, lanes]
        too_low = jnp.min(work, axis=0, keepdims=True) < _FAST_FLOOR
        for r in range(rounds):
            m = jnp.max(work, axis=0, keepdims=True)
            if vals_ref is not None:
                vals_ref[r:r + 1, lanes] = m
            work = jnp.where(work == m, _MARK_BASE * (1.0 + r * _MARK_STEP), work)
        hit = work <= _MARK_BASE
        ranks.append(jnp.where(hit, (work * (1.0 / _MARK_BASE) - 1.0) * (1.0 / _MARK_STEP), float(rounds)))
        miscount = jnp.abs(jnp.sum(hit.astype(F32), axis=0, keepdims=True) - float(rounds))
        bad.append(jnp.where(too_low, 1.0, miscount))
    ok = jnp.max(jnp.concatenate(bad, axis=1)) == 0.0
    return jnp.concatenate(ranks, axis=1), ok


def _extract_exact(work, rounds, vals_ref=None):
    rows = work.shape[0]
    iota = lax.broadcasted_iota(jnp.int32, work.shape, 0).astype(F32)
    rank = jnp.full(work.shape, float(rounds), F32)
    for r in range(rounds):
        m = jnp.max(work, axis=0, keepdims=True)
        idx = jnp.min(jnp.where(work == m, iota, float(rows)), axis=0, keepdims=True)
        hit = iota == idx
        rank = jnp.where(hit, float(r), rank)
        work = jnp.where(hit, -jnp.inf, work)
        if vals_ref is not None:
            vals_ref[r:r + 1, :] = m
    return rank


def _bcast_rows_bf16(row, rows):
    packed = jnp.broadcast_to(row, (BF16_SUBLANES, row.shape[1])).astype(BF16)
    return jnp.concatenate([packed] * (rows // BF16_SUBLANES), axis=0)


def _peer_kernel(xnt_ref, h1_ref, wq_ref, k1_ref, k2_ref, u_first_ref, u_odd_ref, u_even_ref,
                 vt_odd_ref, vt_even_ref, vt_last_ref, fw_ref, out_ref,
                 e1_ref, n1_ref, r2_ref, e2_ref, rk1_ref, rk2_ref, v1_ref, v2_ref, cand_ref, sel_ref, yt_ref,
                 h_even_ref, h_odd_ref, p_even_ref, p_odd_ref, q_ref):
    ib = pl.program_id(1)
    nib = pl.num_programs(1)
    t = xnt_ref.shape[1]

    @pl.when(ib == 0)
    def _():
        yt_ref[...] = jnp.zeros(yt_ref.shape, F32)
        p_odd_ref[...] = jnp.zeros(p_odd_ref.shape, BF16)
        xnt = xnt_ref[...]
        h_even_ref[...] = jnp.dot(u_first_ref[...], xnt, preferred_element_type=F32)
        q_ref[...] = jnp.dot(wq_ref[...], xnt, preferred_element_type=F32).astype(BF16)

        def head(h, carry):
            r0 = pl.multiple_of(h * 2 * PEER_HALF, 2 * PEER_HALF)
            s1 = jnp.dot(k1_ref[h], q_ref[pl.ds(r0, PEER_HALF), :], preferred_element_type=F32)
            s2 = jnp.dot(k2_ref[h], q_ref[pl.ds(r0 + PEER_HALF, PEER_HALF), :], preferred_element_type=F32)
            rank1, ok1 = _extract_fast(s1, PEER_TOPK, v1_ref)
            rank2, ok2 = _extract_fast(s2, PEER_TOPK, v2_ref)
            rk1_ref[...] = rank1
            rk2_ref[...] = rank2

            @pl.when(jnp.logical_not(ok1 & ok2))
            def _():
                rk1_ref[...] = _extract_exact(s1, PEER_TOPK, v1_ref)
                rk2_ref[...] = _extract_exact(s2, PEER_TOPK, v2_ref)

            cand_ref[...] = jnp.full(cand_ref.shape, -jnp.inf, F32)
            for n, (a, b) in enumerate(PEER_CANDS):
                cand_ref[n:n + 1, :] = v1_ref[a:a + 1, :] + v2_ref[b:b + 1, :]
            cand = cand_ref[...]
            valid = lax.broadcasted_iota(jnp.int32, cand.shape, 0) < len(PEER_CANDS)
            crank, ok3 = _extract_fast(jnp.where(valid, cand, _FAST_FLOOR), PEER_TOPK)
            sel_ref[...] = crank

            @pl.when(jnp.logical_not(ok3))
            def _():
                sel_ref[...] = _extract_exact(cand, PEER_TOPK)

            sel = sel_ref[...] < float(PEER_TOPK)
            cexp = jnp.where(sel, jnp.exp(jnp.where(sel, cand - cand[0:1, :], 0.0)), 0.0)
            inv_z = 1.0 / jnp.sum(cexp, axis=0, keepdims=True)
            self32 = sel.astype(F32)
            rank1 = rk1_ref[...].astype(BF16)
            n1 = jnp.zeros(rank1.shape, BF16)
            n = 0
            for a in range(PEER_TOPK):
                width = PEER_TOPK // (a + 1)
                n_a = jnp.sum(self32[n:n + width, :], axis=0, keepdims=True)
                n1 = jnp.where(rank1 == jnp.asarray(a, BF16), _bcast_rows_bf16(n_a, N_KEYS), n1)
                n += width
            e1_ref[h] = jnp.exp(s1 - v1_ref[0:1, :]) * (0.5 * inv_z)
            n1_ref[h] = n1.astype(F32)
            r2_ref[h] = rk2_ref[...].astype(BF16)
            e2_ref[h] = jnp.exp(s2 - v2_ref[0:1, :]).astype(BF16)
            return carry

        lax.fori_loop(0, PEER_HEADS, head, 0)

    th = t // 2

    def stage(blk, hf, u_next_ref, h_next_ref, h_cur_ref, p_cur_ref, vt_prev_ref, p_prev_ref):
        lanes = slice(hf * th, (hf + 1) * th)
        h_next_ref[:, lanes] = jnp.dot(u_next_ref[...], xnt_ref[:, lanes], preferred_element_type=F32)
        for kk in range(PEER_IBLK):
            i = blk * PEER_IBLK + kk
            rows = slice(kk * N_KEYS, (kk + 1) * N_KEYS)
            hk = h_cur_ref[rows, lanes]
            act = hk * (1.0 + lax.erf(hk * 0.7071067811865476))
            g = jnp.zeros((N_KEYS, th), BF16)
            for h in range(PEER_HEADS):
                e1row = _bcast_rows_bf16(e1_ref[h, pl.ds(i, 1), lanes], N_KEYS)
                n1row = _bcast_rows_bf16(n1_ref[h, pl.ds(i, 1), lanes], N_KEYS)
                g = g + e1row * jnp.where(r2_ref[h, :, lanes] < n1row, e2_ref[h, :, lanes], jnp.zeros((), BF16))
            p_cur_ref[rows, lanes] = g * act.astype(BF16)
        yt_ref[:, lanes] += jnp.dot(vt_prev_ref[...], p_prev_ref[:, lanes], preferred_element_type=F32)

    for hf in range(2):
        stage(2 * ib, hf, u_odd_ref, h_odd_ref, h_even_ref, p_even_ref, vt_odd_ref, p_odd_ref)
    for hf in range(2):
        stage(2 * ib + 1, hf, u_even_ref, h_even_ref, h_odd_ref, p_odd_ref, vt_even_ref, p_even_ref)

    @pl.when(ib == nib - 1)
    def _():
        yt = yt_ref[...] + jnp.dot(vt_last_ref[...], p_odd_ref[...], preferred_element_type=F32)
        hfin = h1_ref[...] + yt.T
        out_ref[...] = _rms(hfin, fw_ref[...])


def _peer(xn2t, h1, wqt, k1, k2, u, vt, fw):
    n = xn2t.shape[1]
    t = PEER_TILE
    eb = PEER_IBLK * N_KEYS
    nblk = (N_KEYS * N_KEYS) // eb
    assert nblk % 2 == 0
    tok = pl.BlockSpec((t, D_MODEL), lambda i, j: (i, 0))
    big = pltpu.VMEM((PEER_HEADS, N_KEYS, t), F32)
    bigb = pltpu.VMEM((PEER_HEADS, N_KEYS, t), BF16)
    keys = pltpu.VMEM((N_KEYS, t), F32)
    u_spec = lambda f: pl.BlockSpec((eb, D_MODEL), lambda i, j: (f(j), 0))
    vt_spec = lambda f: pl.BlockSpec((D_MODEL, eb), lambda i, j: (0, f(j)))
    return pl.pallas_call(
        _peer_kernel,
        grid=(n // t, nblk // 2),
        in_specs=[pl.BlockSpec((D_MODEL, t), lambda i, j: (0, i)), tok,
                  _const_spec(wqt.shape), _const_spec(k1.shape), _const_spec(k2.shape),
                  u_spec(lambda j: 0), u_spec(lambda j: 2 * j + 1),
                  u_spec(lambda j: jnp.minimum(2 * j + 2, nblk - 1)),
                  vt_spec(lambda j: jnp.maximum(2 * j - 1, 0)), vt_spec(lambda j: 2 * j),
                  vt_spec(lambda j: nblk - 1),
                  _const_spec(fw.shape)],
        out_specs=tok,
        out_shape=jax.ShapeDtypeStruct((n, D_MODEL), F32),
        scratch_shapes=[big, big, bigb, bigb, keys, keys,
                        pltpu.VMEM((PEER_TOPK, t), F32), pltpu.VMEM((PEER_TOPK, t), F32),
                        pltpu.VMEM((PEER_CAND_ROWS, t), F32), pltpu.VMEM((PEER_CAND_ROWS, t), F32),
                        pltpu.VMEM((D_MODEL, t), F32),
                        pltpu.VMEM((eb, t), F32), pltpu.VMEM((eb, t), F32),
                        pltpu.VMEM((eb, t), BF16), pltpu.VMEM((eb, t), BF16),
                        pltpu.VMEM((2 * PEER_HALF * PEER_HEADS, t), BF16)],
        compiler_params=_params("parallel", "arbitrary"),
        name="peer",
    )(xn2t, h1, wqt, k1, k2, u, u, u, vt, vt, vt, fw)


def kernel(x, mix_norm_w, w_in, pool_w, pool_scale, conv_w, a_log, dt_bias, dn_norm_w, w_pool_up,
           w_dn_up, w_mix_out, ffn_norm_w, peer_w_query, peer_keys_1, peer_keys_2, peer_down, peer_up,
           final_norm_w):
    b, s, d = x.shape
    n = b * s
    assert d == D_MODEL and s % POOL_TILE == 0 and s % DN_STEP == 0
    assert n % IN_TILE == 0 and n % POST_TILE == 0 and n % PEER_TILE == 0
    assert w_in.shape[0] == 1, "single-layer block"
    l = 0
    h = x.reshape(n, d)

    c0 = POOL_WIDTH
    c1 = c0 + 3 * DN_WIDTH
    c2 = c1 + DN_WIDTH
    c3 = c2 + 2 * DN_HEADS
    c4 = c3 + D_MODEL
    wi = w_in[l]
    xa, qkv, z, ga, gb, bd = _in_proj(
        h, mix_norm_w[l].reshape(1, d),
        wi[:, :c0].astype(BF16), wi[:, c0:c1].astype(BF16), wi[:, c1:c2].astype(BF16),
        wi[:, c3:c4].astype(BF16), wi[:, c4:].astype(BF16), wi[:, c2:c3].astype(BF16))

    ya = _pool(xa.reshape(b, s, POOL_WIDTH), pool_w[l].astype(BF16), pool_scale[l].reshape(1, POOL_WIDTH),
               w_pool_up[l].astype(BF16))

    bd3 = bd.reshape(b, s, 2 * DN_HEADS)
    o = _deltanet(qkv.reshape(b, s, 3 * DN_WIDTH), bd3, bd3.transpose(0, 2, 1), conv_w[l], a_log[l],
                  dt_bias[l])

    h1, xn2t = _post(o.reshape(n, DN_WIDTH), z, ga, gb, ya.reshape(n, d), h,
                    dn_norm_w[l].reshape(1, DN_HEAD_DIM), w_dn_up[l].astype(BF16),
                    w_mix_out[l].astype(BF16), ffn_norm_w[l].reshape(1, d))

    out = _peer(xn2t, h1, peer_w_query[l].T.astype(BF16), peer_keys_1[l].astype(BF16),
                peer_keys_2[l].astype(BF16), peer_down[l].astype(BF16), peer_up[l].T.astype(BF16),
                final_norm_w.reshape(1, d))
    return out.reshape(b, s, d)
```

```python
import functools

import jax
import jax.numpy as jnp
from jax import lax
from jax.experimental import pallas as pl
from jax.experimental.pallas import tpu as pltpu

F32 = jnp.float32
BF16 = jnp.bfloat16
EPS = 1e-6

D_MODEL = 1024
POOL_WINDOWS = (2, 4, 8, 16)
POOL_GROUP_DIM = 128
POOL_WIDTH = 512
POOL_HALO = 16
DN_HEADS = 8
DN_HEAD_DIM = 128
DN_WIDTH = DN_HEADS * DN_HEAD_DIM
CONV_WIDTH = 4
CONV_HALO = 8
DN_STEP = 1024
DN_CHUNK = 128
PEER_HEADS = 8
N_KEYS = 128
PEER_TOPK = 16
PEER_HALF = 128

V7X_LANES = 128
BF16_SUBLANES = 16
VMEM_LIMIT = 56 * 1024 * 1024

IN_TILE = 512
POST_TILE = 1024
PEER_TILE = 512
PEER_IBLK = 4

PEER_CANDS = tuple((a, b) for a in range(PEER_TOPK) for b in range(PEER_TOPK)
                   if (a + 1) * (b + 1) <= PEER_TOPK)
PEER_CAND_ROWS = 56

_NT = (((1,), (1,)), ((), ()))
_TN = (((0,), (0,)), ((), ()))


def _params(*sem):
    return pltpu.CompilerParams(dimension_semantics=sem, vmem_limit_bytes=VMEM_LIMIT)


def _const_spec(shape):
    return pl.BlockSpec(shape, lambda *_: (0,) * len(shape))


def _resident_spec(shape):
    return pl.BlockSpec(shape, lambda *_: (0,) * len(shape), pipeline_mode=pl.Buffered(1))


def _sigmoid(x):
    return 1.0 / (1.0 + jnp.exp(-x))


def _in_proj_kernel(x_ref, nw_ref, wxa_ref, wqkv_ref, wz_ref, wga_ref, wgb_ref, wbd_ref,
                    xa_ref, qkv_ref, z_ref, ga_ref, gb_ref, bd_ref):
    x = x_ref[...]
    xn = x * lax.rsqrt(jnp.mean(x * x, axis=-1, keepdims=True) + EPS) * nw_ref[...]
    xb = xn.astype(BF16)
    xa_ref[...] = jnp.dot(xb, wxa_ref[...], preferred_element_type=F32)
    qkv_ref[...] = jnp.dot(xb, wqkv_ref[...], preferred_element_type=F32)
    z_ref[...] = jnp.dot(xb, wz_ref[...], preferred_element_type=F32).astype(BF16)
    ga_ref[...] = jnp.dot(xb, wga_ref[...], preferred_element_type=F32).astype(BF16)
    gb_ref[...] = jnp.dot(xb, wgb_ref[...], preferred_element_type=F32).astype(BF16)
    bd_ref[...] = jnp.dot(xb, wbd_ref[...], preferred_element_type=F32)


def _in_proj(x2, nw, wxa, wqkv, wz, wga, wgb, wbd):
    n = x2.shape[0]
    t = IN_TILE
    row = lambda w: pl.BlockSpec((t, w), lambda i: (i, 0))
    widths = (POOL_WIDTH, 3 * DN_WIDTH, DN_WIDTH, D_MODEL, D_MODEL, 2 * DN_HEADS)
    dtypes = (F32, F32, BF16, BF16, BF16, F32)
    return pl.pallas_call(
        _in_proj_kernel,
        grid=(n // t,),
        in_specs=[row(D_MODEL), _const_spec(nw.shape), _resident_spec(wxa.shape), _resident_spec(wqkv.shape),
                  _resident_spec(wz.shape), _resident_spec(wga.shape), _resident_spec(wgb.shape),
                  _const_spec(wbd.shape)],
        out_specs=[row(w) for w in widths],
        out_shape=[jax.ShapeDtypeStruct((n, w), dt) for w, dt in zip(widths, dtypes)],
        compiler_params=_params("parallel"),
        name="in_proj",
    )(x2, nw, wxa, wqkv, wz, wga, wgb, wbd)


def _softplus(x):
    return jnp.maximum(x, 0.0) + jnp.log1p(jnp.exp(-jnp.abs(x)))


def _dn_kernel(qkv_ref, bd_ref, bdt_ref, cw_ref, alog_ref, dtb_ref, alogt_ref, dtbt_ref,
               o_ref, cbuf_ref, st_ref):
    s = pl.program_id(1)
    c = DN_CHUNK
    hd = DN_HEAD_DIM

    @pl.when(s == 0)
    def _():
        cbuf_ref[0:CONV_HALO, :] = jnp.zeros((CONV_HALO, 3 * DN_WIDTH), F32)
        st_ref[...] = jnp.zeros(st_ref.shape, F32)

    @pl.when(s > 0)
    def _():
        cbuf_ref[0:CONV_HALO, :] = cbuf_ref[DN_STEP:DN_STEP + CONV_HALO, :]

    cbuf_ref[CONV_HALO:, :] = qkv_ref[0]
    for r0 in range(0, DN_STEP, c):
        _dn_chunk(r0, bd_ref, bdt_ref, cw_ref, alog_ref, dtb_ref, alogt_ref, dtbt_ref, o_ref, cbuf_ref, st_ref)


def _dn_chunk(r0, bd_ref, bdt_ref, cw_ref, alog_ref, dtb_ref, alogt_ref, dtbt_ref, o_ref, cbuf_ref, st_ref):
    c = DN_CHUNK
    hd = DN_HEAD_DIM
    bd = bd_ref[0, r0:r0 + c, :]
    bdt = bdt_ref[0, :, r0:r0 + c]
    beta_all = _sigmoid(bd[:, 0:DN_HEADS])
    g_all = -jnp.exp(alog_ref[...]) * _softplus(bd[:, DN_HEADS:] + dtb_ref[...])
    g_allt = -jnp.exp(alogt_ref[...]) * _softplus(bdt[DN_HEADS:, :] + dtbt_ref[...])

    row = lax.broadcasted_iota(jnp.int32, (c, c), 0)
    col = lax.broadcasted_iota(jnp.int32, (c, c), 1)
    tril = row >= col
    strict = row > col
    gc = jnp.dot(tril.astype(F32), g_all, preferred_element_type=F32, precision=lax.Precision.HIGHEST)
    gct = jnp.dot(g_allt, (row <= col).astype(F32), preferred_element_type=F32,
                  precision=lax.Precision.HIGHEST)
    eye = (row == col).astype(F32)

    def conv_silu(lane0):
        sl = slice(lane0, lane0 + hd)
        y = cw_ref[0:1, sl] * cbuf_ref[r0 + CONV_HALO - 3:r0 + CONV_HALO - 3 + c, sl]
        for j in range(1, CONV_WIDTH):
            y = y + cw_ref[j:j + 1, sl] * cbuf_ref[r0 + CONV_HALO - 3 + j:r0 + CONV_HALO - 3 + j + c, sl]
        return y * _sigmoid(y)

    def l2n(v):
        return v * lax.rsqrt(jnp.sum(v * v, axis=-1, keepdims=True) + EPS)

    hs = range(DN_HEADS)
    q = [l2n(conv_silu(h * hd)) * (hd ** -0.5) for h in hs]
    k = [l2n(conv_silu(DN_WIDTH + h * hd)) for h in hs]
    v = [conv_silu(2 * DN_WIDTH + h * hd) for h in hs]
    gcol = [gc[:, h:h + 1] for h in hs]
    grow = [gct[h:h + 1, :] for h in hs]
    decay = [jnp.where(tril, jnp.exp(jnp.where(tril, gcol[h] - grow[h], 0.0)), 0.0) for h in hs]
    kb = [k[h] * beta_all[:, h:h + 1] for h in hs]
    vb = [v[h] * beta_all[:, h:h + 1] for h in hs]
    kf = [k[h].astype(BF16) for h in hs]
    lower = [jnp.where(strict, lax.dot_general(kb[h].astype(BF16), kf[h], _NT, preferred_element_type=F32)
                       * decay[h], 0.0) for h in hs]
    attn = [(lax.dot_general(q[h].astype(BF16), kf[h], _NT, preferred_element_type=F32) * decay[h]).astype(BF16)
            for h in hs]
    pair = ((row ^ col) == 1) & strict
    tinv = [eye - jnp.where(pair, lower[h], 0.0) for h in hs]
    blk = 2
    while blk < c:
        m = ((row ^ col) < 2 * blk) & ((row & blk) != 0) & ((col & blk) == 0)
        tb = [tinv[h].astype(BF16) for h in hs]
        left = [jnp.dot(tb[h], jnp.where(m, lower[h], 0.0).astype(BF16), preferred_element_type=F32).astype(BF16)
                for h in hs]
        tinv = [tinv[h] - jnp.dot(left[h], tb[h], preferred_element_type=F32) for h in hs]
        blk *= 2
    eg = [jnp.exp(gcol[h]) for h in hs]
    uw = [jnp.dot(tinv[h].astype(BF16),
                  jnp.concatenate([vb[h], kb[h] * eg[h]], axis=-1).astype(BF16), preferred_element_type=F32)
          for h in hs]
    st = [st_ref[h] for h in hs]
    ws = [jnp.dot(jnp.concatenate([uw[h][:, hd:], q[h] * eg[h]], axis=0).astype(BF16), st[h].astype(BF16),
                  preferred_element_type=F32) for h in hs]
    vnb = [(uw[h][:, :hd] - ws[h][:c]).astype(BF16) for h in hs]
    o = [ws[h][c:] + jnp.dot(attn[h], vnb[h], preferred_element_type=F32) for h in hs]
    for h in hs:
        g_last = grow[h][:, c - 1:c]
        k_dec = (k[h] * jnp.exp(g_last - gcol[h])).astype(BF16)
        st_ref[h] = st[h] * jnp.exp(g_last) + lax.dot_general(k_dec, vnb[h], _TN, preferred_element_type=F32)
    o_ref[0, r0:r0 + c, :] = jnp.concatenate(o, axis=-1).astype(BF16)


def _deltanet(qkv, bd, bdt, cw, alog, dtb):
    b, s, _ = qkv.shape
    c = DN_STEP
    alog2, dtb2 = alog.reshape(1, DN_HEADS), dtb.reshape(1, DN_HEADS)
    alogt, dtbt = alog.reshape(DN_HEADS, 1), dtb.reshape(DN_HEADS, 1)
    return pl.pallas_call(
        _dn_kernel,
        grid=(b, s // c),
        in_specs=[pl.BlockSpec((1, c, 3 * DN_WIDTH), lambda i, j: (i, j, 0)),
                  pl.BlockSpec((1, c, 2 * DN_HEADS), lambda i, j: (i, j, 0)),
                  pl.BlockSpec((1, 2 * DN_HEADS, c), lambda i, j: (i, 0, j)),
                  _const_spec(cw.shape), _const_spec(alog2.shape), _const_spec(dtb2.shape),
                  _const_spec(alogt.shape), _const_spec(dtbt.shape)],
        out_specs=pl.BlockSpec((1, c, DN_WIDTH), lambda i, j: (i, j, 0)),
        out_shape=jax.ShapeDtypeStruct((b, s, DN_WIDTH), BF16),
        scratch_shapes=[pltpu.VMEM((c + CONV_HALO, 3 * DN_WIDTH), F32),
                        pltpu.VMEM((DN_HEADS, DN_HEAD_DIM, DN_HEAD_DIM), F32)],
        compiler_params=_params("parallel", "arbitrary"),
        name="deltanet",
    )(qkv, bd, bdt, cw, alog2, dtb2, alogt, dtbt)


def _rms(x, w):
    return x * lax.rsqrt(jnp.mean(x * x, axis=-1, keepdims=True) + EPS) * w


def _post_kernel(tiles_per_seq, o_ref, z_ref, ga_ref, gb_ref, xa_ref, x_ref, dnw_ref, wdn_ref, wmix_ref, fnw_ref,
                 pw_ref, ps_ref, wup_ref, h1_ref, xn2t_ref, buf_ref):
    t = x_ref.shape[0]
    seq_tile = pl.program_id(0) % tiles_per_seq

    @pl.when(seq_tile == 0)
    def _():
        buf_ref[0:POOL_HALO, :] = jnp.zeros((POOL_HALO, POOL_WIDTH), F32)

    @pl.when(seq_tile > 0)
    def _():
        buf_ref[0:POOL_HALO, :] = buf_ref[t:t + POOL_HALO, :]

    buf_ref[POOL_HALO:, :] = xa_ref[...]
    pos = seq_tile * t + lax.broadcasted_iota(jnp.int32, (t, 1), 0)
    ys = []
    for g, win in enumerate(POOL_WINDOWS):
        sl = slice(g * POOL_GROUP_DIM, (g + 1) * POOL_GROUP_DIM)
        xg = buf_ref[POOL_HALO:POOL_HALO + t, sl]
        acc = xg
        for k in range(1, win):
            acc = acc + buf_ref[POOL_HALO - k:POOL_HALO - k + t, sl]
        pooled = acc / jnp.minimum(pos + 1, win).astype(F32) - xg
        ys.append(jnp.dot(pooled.astype(BF16), pw_ref[g], preferred_element_type=F32) * ps_ref[:, sl])
    ya = jnp.dot(jnp.concatenate(ys, axis=-1).astype(BF16), wup_ref[...], preferred_element_type=F32)

    o = o_ref[...].astype(F32)
    z = z_ref[...].astype(F32)
    parts = []
    for h in range(DN_HEADS):
        sl = slice(h * DN_HEAD_DIM, (h + 1) * DN_HEAD_DIM)
        parts.append(_rms(o[:, sl], dnw_ref[...]))
    on = jnp.concatenate(parts, axis=-1) * (z * _sigmoid(z))
    yb = jnp.dot(on.astype(BF16), wdn_ref[...], preferred_element_type=F32)
    merged = _sigmoid(ga_ref[...].astype(F32)) * ya + _sigmoid(gb_ref[...].astype(F32)) * yb
    h1 = x_ref[...] + jnp.dot(merged.astype(BF16), wmix_ref[...], preferred_element_type=F32)
    h1_ref[...] = h1
    xn2t_ref[...] = _rms(h1, fnw_ref[...]).T.astype(BF16)


def _post(o, z, ga, gb, xa, x2, seq, dnw, wdn, wmix, fnw, pw, ps, wup):
    n = x2.shape[0]
    t = POST_TILE
    assert seq % t == 0
    row = pl.BlockSpec((t, D_MODEL), lambda i: (i, 0))
    return pl.pallas_call(
        functools.partial(_post_kernel, seq // t),
        grid=(n // t,),
        in_specs=[row, row, row, row, pl.BlockSpec((t, POOL_WIDTH), lambda i: (i, 0)), row,
                  _const_spec(dnw.shape), _const_spec(wdn.shape), _const_spec(wmix.shape), _const_spec(fnw.shape),
                  _const_spec(pw.shape), _const_spec(ps.shape), _const_spec(wup.shape)],
        out_specs=[row, pl.BlockSpec((D_MODEL, t), lambda i: (0, i))],
        out_shape=[jax.ShapeDtypeStruct((n, D_MODEL), F32), jax.ShapeDtypeStruct((D_MODEL, n), BF16)],
        scratch_shapes=[pltpu.VMEM((t + POOL_HALO, POOL_WIDTH), F32)],
        compiler_params=_params("arbitrary"),
        name="post",
    )(o, z, ga, gb, xa, x2, dnw, wdn, wmix, fnw, pw, ps, wup)


_MARK_BASE = -2.0 ** 120
_MARK_STEP = 1.0 / 32.0
_FAST_FLOOR = -2.0 ** 119


def _extract_fast(x, rounds, vals_ref=None):
    ranks, bad = [], []
    for l0 in range(0, x.shape[1], V7X_LANES):
        lanes = slice(l0, l0 + V7X_LANES)
        work = x[:, lanes]
        too_low = jnp.min(work, axis=0, keepdims=True) < _FAST_FLOOR
        for r in range(rounds):
            m = jnp.max(work, axis=0, keepdims=True)
            if vals_ref is not None:
                vals_ref[r:r + 1, lanes] = m
            work = jnp.where(work == m, _MARK_BASE * (1.0 + r * _MARK_STEP), work)
        hit = work <= _MARK_BASE
        ranks.append(jnp.where(hit, (work * (1.0 / _MARK_BASE) - 1.0) * (1.0 / _MARK_STEP), float(rounds)))
        miscount = jnp.abs(jnp.sum(hit.astype(F32), axis=0, keepdims=True) - float(rounds))
        bad.append(jnp.where(too_low, 1.0, miscount))
    ok = jnp.max(jnp.concatenate(bad, axis=1)) == 0.0
    return jnp.concatenate(ranks, axis=1), ok


def _extract_exact(work, rounds, vals_ref=None):
    rows = work.shape[0]
    iota = lax.broadcasted_iota(jnp.int32, work.shape, 0).astype(F32)
    rank = jnp.full(work.shape, float(rounds), F32)
    for r in range(rounds):
        m = jnp.max(work, axis=0, keepdims=True)
        idx = jnp.min(jnp.where(work == m, iota, float(rows)), axis=0, keepdims=True)
        hit = iota == idx
        rank = jnp.where(hit, float(r), rank)
        work = jnp.where(hit, -jnp.inf, work)
        if vals_ref is not None:
            vals_ref[r:r + 1, :] = m
    return rank


def _bcast_rows_bf16(row, rows):
    packed = jnp.broadcast_to(row, (BF16_SUBLANES, row.shape[1])).astype(BF16)
    return jnp.concatenate([packed] * (rows // BF16_SUBLANES), axis=0)


def _peer_kernel(xnt_ref, h1_ref, wq_ref, k1_ref, k2_ref, u_first_ref, u_odd_ref, u_even_ref,
                 vt_odd_ref, vt_even_ref, vt_last_ref, fw_ref, out_ref,
                 e1_ref, n1_ref, r2_ref, e2_ref, rk1_ref, rk2_ref, v1_ref, v2_ref, cand_ref, sel_ref, yt_ref,
                 h_even_ref, h_odd_ref, p_even_ref, p_odd_ref, q_ref):
    ib = pl.program_id(1)
    nib = pl.num_programs(1)
    t = xnt_ref.shape[1]

    @pl.when(ib == 0)
    def _():
        yt_ref[...] = jnp.zeros(yt_ref.shape, F32)
        p_odd_ref[...] = jnp.zeros(p_odd_ref.shape, BF16)
        xnt = xnt_ref[...]
        h_even_ref[...] = jnp.dot(u_first_ref[...], xnt, preferred_element_type=F32)
        q_ref[...] = jnp.dot(wq_ref[...], xnt, preferred_element_type=F32).astype(BF16)

        def head(h, carry):
            r0 = pl.multiple_of(h * 2 * PEER_HALF, 2 * PEER_HALF)
            s1 = jnp.dot(k1_ref[h], q_ref[pl.ds(r0, PEER_HALF), :], preferred_element_type=F32)
            s2 = jnp.dot(k2_ref[h], q_ref[pl.ds(r0 + PEER_HALF, PEER_HALF), :], preferred_element_type=F32)
            rank1, ok1 = _extract_fast(s1, PEER_TOPK, v1_ref)
            rank2, ok2 = _extract_fast(s2, PEER_TOPK, v2_ref)
            rk1_ref[...] = rank1
            rk2_ref[...] = rank2

            @pl.when(jnp.logical_not(ok1 & ok2))
            def _():
                rk1_ref[...] = _extract_exact(s1, PEER_TOPK, v1_ref)
                rk2_ref[...] = _extract_exact(s2, PEER_TOPK, v2_ref)

            cand_ref[...] = jnp.full(cand_ref.shape, -jnp.inf, F32)
            for n, (a, b) in enumerate(PEER_CANDS):
                cand_ref[n:n + 1, :] = v1_ref[a:a + 1, :] + v2_ref[b:b + 1, :]
            cand = cand_ref[...]
            valid = lax.broadcasted_iota(jnp.int32, cand.shape, 0) < len(PEER_CANDS)
            crank, ok3 = _extract_fast(jnp.where(valid, cand, _FAST_FLOOR), PEER_TOPK)
            sel_ref[...] = crank

            @pl.when(jnp.logical_not(ok3))
            def _():
                sel_ref[...] = _extract_exact(cand, PEER_TOPK)

            sel = sel_ref[...] < float(PEER_TOPK)
            cexp = jnp.where(sel, jnp.exp(jnp.where(sel, cand - cand[0:1, :], 0.0)), 0.0)
            inv_z = 1.0 / jnp.sum(cexp, axis=0, keepdims=True)
            self32 = sel.astype(F32)
            rank1 = rk1_ref[...].astype(BF16)
            n1 = jnp.zeros(rank1.shape, BF16)
            n = 0
            for a in range(PEER_TOPK):
                width = PEER_TOPK // (a + 1)
                n_a = jnp.sum(self32[n:n + width, :], axis=0, keepdims=True)
                n1 = jnp.where(rank1 == jnp.asarray(a, BF16), _bcast_rows_bf16(n_a, N_KEYS), n1)
                n += width
            e1_ref[h] = jnp.exp(s1 - v1_ref[0:1, :]) * (0.5 * inv_z)
            n1_ref[h] = n1.astype(F32)
            r2_ref[h] = rk2_ref[...].astype(BF16)
            e2_ref[h] = jnp.exp(s2 - v2_ref[0:1, :]).astype(BF16)
            return carry

        lax.fori_loop(0, PEER_HEADS, head, 0)

    th = t // 2

    def stage(blk, hf, u_next_ref, h_next_ref, h_cur_ref, p_cur_ref, vt_prev_ref, p_prev_ref):
        lanes = slice(hf * th, (hf + 1) * th)
        h_next_ref[:, lanes] = jnp.dot(u_next_ref[...], xnt_ref[:, lanes], preferred_element_type=F32)
        for kk in range(PEER_IBLK):
            i = blk * PEER_IBLK + kk
            rows = slice(kk * N_KEYS, (kk + 1) * N_KEYS)
            hk = h_cur_ref[rows, lanes]
            act = hk * (1.0 + lax.erf(hk * 0.7071067811865476))
            g = jnp.zeros((N_KEYS, th), BF16)
            for h in range(PEER_HEADS):
                e1row = _bcast_rows_bf16(e1_ref[h, pl.ds(i, 1), lanes], N_KEYS)
                n1row = _bcast_rows_bf16(n1_ref[h, pl.ds(i, 1), lanes], N_KEYS)
                g = g + e1row * jnp.where(r2_ref[h, :, lanes] < n1row, e2_ref[h, :, lanes], jnp.zeros((), BF16))
            p_cur_ref[rows, lanes] = g * act.astype(BF16)
        yt_ref[:, lanes] += jnp.dot(vt_prev_ref[...], p_prev_ref[:, lanes], preferred_element_type=F32)

    for hf in range(2):
        stage(2 * ib, hf, u_odd_ref, h_odd_ref, h_even_ref, p_even_ref, vt_odd_ref, p_odd_ref)
    for hf in range(2):
        stage(2 * ib + 1, hf, u_even_ref, h_even_ref, h_odd_ref, p_odd_ref, vt_even_ref, p_even_ref)

    @pl.when(ib == nib - 1)
    def _():
        yt = yt_ref[...] + jnp.dot(vt_last_ref[...], p_odd_ref[...], preferred_element_type=F32)
        hfin = h1_ref[...] + yt.T
        out_ref[...] = _rms(hfin, fw_ref[...])


def _peer(xn2t, h1, wqt, k1, k2, u, vt, fw):
    n = xn2t.shape[1]
    t = PEER_TILE
    eb = PEER_IBLK * N_KEYS
    nblk = (N_KEYS * N_KEYS) // eb
    assert nblk % 2 == 0
    tok = pl.BlockSpec((t, D_MODEL), lambda i, j: (i, 0))
    big = pltpu.VMEM((PEER_HEADS, N_KEYS, t), F32)
    bigb = pltpu.VMEM((PEER_HEADS, N_KEYS, t), BF16)
    keys = pltpu.VMEM((N_KEYS, t), F32)
    u_spec = lambda f: pl.BlockSpec((eb, D_MODEL), lambda i, j: (f(j), 0))
    vt_spec = lambda f: pl.BlockSpec((D_MODEL, eb), lambda i, j: (0, f(j)))
    return pl.pallas_call(
        _peer_kernel,
        grid=(n // t, nblk // 2),
        in_specs=[pl.BlockSpec((D_MODEL, t), lambda i, j: (0, i)), tok,
                  _const_spec(wqt.shape), _const_spec(k1.shape), _const_spec(k2.shape),
                  u_spec(lambda j: 0), u_spec(lambda j: 2 * j + 1),
                  u_spec(lambda j: jnp.minimum(2 * j + 2, nblk - 1)),
                  vt_spec(lambda j: jnp.maximum(2 * j - 1, 0)), vt_spec(lambda j: 2 * j),
                  vt_spec(lambda j: nblk - 1),
                  _const_spec(fw.shape)],
        out_specs=tok,
        out_shape=jax.ShapeDtypeStruct((n, D_MODEL), F32),
        scratch_shapes=[big, big, bigb, bigb, keys, keys,
                        pltpu.VMEM((PEER_TOPK, t), F32), pltpu.VMEM((PEER_TOPK, t), F32),
                        pltpu.VMEM((PEER_CAND_ROWS, t), F32), pltpu.VMEM((PEER_CAND_ROWS, t), F32),
                        pltpu.VMEM((D_MODEL, t), F32),
                        pltpu.VMEM((eb, t), F32), pltpu.VMEM((eb, t), F32),
                        pltpu.VMEM((eb, t), BF16), pltpu.VMEM((eb, t), BF16),
                        pltpu.VMEM((2 * PEER_HALF * PEER_HEADS, t), BF16)],
        compiler_params=_params("parallel", "arbitrary"),
        name="peer",
    )(xn2t, h1, wqt, k1, k2, u, u, u, vt, vt, vt, fw)


def kernel(x, mix_norm_w, w_in, pool_w, pool_scale, conv_w, a_log, dt_bias, dn_norm_w, w_pool_up,
           w_dn_up, w_mix_out, ffn_norm_w, peer_w_query, peer_keys_1, peer_keys_2, peer_down, peer_up,
           final_norm_w):
    b, s, d = x.shape
    n = b * s
    assert d == D_MODEL and s % POST_TILE == 0 and s % DN_STEP == 0
    assert n % IN_TILE == 0 and n % POST_TILE == 0 and n % PEER_TILE == 0
    assert w_in.shape[0] == 1, "single-layer block"
    l = 0
    h = x.reshape(n, d)

    c0 = POOL_WIDTH
    c1 = c0 + 3 * DN_WIDTH
    c2 = c1 + DN_WIDTH
    c3 = c2 + 2 * DN_HEADS
    c4 = c3 + D_MODEL
    wi = w_in[l]
    xa, qkv, z, ga, gb, bd = _in_proj(
        h, mix_norm_w[l].reshape(1, d),
        wi[:, :c0].astype(BF16), wi[:, c0:c1].astype(BF16), wi[:, c1:c2].astype(BF16),
        wi[:, c3:c4].astype(BF16), wi[:, c4:].astype(BF16), wi[:, c2:c3].astype(BF16))

    bd3 = bd.reshape(b, s, 2 * DN_HEADS)
    o = _deltanet(qkv.reshape(b, s, 3 * DN_WIDTH), bd3, bd3.transpose(0, 2, 1), conv_w[l], a_log[l],
                  dt_bias[l])

    h1, xn2t = _post(o.reshape(n, DN_WIDTH), z, ga, gb, xa, h, s,
                    dn_norm_w[l].reshape(1, DN_HEAD_DIM), w_dn_up[l].astype(BF16),
                    w_mix_out[l].astype(BF16), ffn_norm_w[l].reshape(1, d),
                    pool_w[l].astype(BF16), pool_scale[l].reshape(1, POOL_WIDTH), w_pool_up[l].astype(BF16))

    out = _peer(xn2t, h1, peer_w_query[l].T.astype(BF16), peer_keys_1[l].astype(BF16),
                peer_keys_2[l].astype(BF16), peer_down[l].astype(BF16), peer_up[l].T.astype(BF16),
                final_norm_w.reshape(1, d))
    return out.reshape(b, s, d)
```
